```python
import math
import jax
import jax.numpy as jnp
from jax import lax
import numpy as np

D_MODEL = 1024
BATCH = 4
SEQ = 4096
DEPTH = 2
DEC_BATCH = 8
DEC_SEQ = 2048
PAST_LEN = 128

GRID_W = 64
CHUNK = 128
A_WIDTH = D_MODEL // 2
A_GROUPS = 4
A_GROUP_DIM = A_WIDTH // A_GROUPS
NA_HEADS = 8
NA_HEAD_DIM = (D_MODEL - A_WIDTH) // NA_HEADS
NA_WIDTH = NA_HEADS * NA_HEAD_DIM
NA_KH_MAX = 8
NA_KW = 16
DIFF_HEADS = 8
DIFF_HEAD_DIM = D_MODEL // DIFF_HEADS // 2
DIFF_WIDTH = DIFF_HEADS * 2 * DIFF_HEAD_DIM
D_FF = 4 * D_MODEL
Q_BLOCK = 128
LN_EPS = 1e-5
DEEPNORM_ALPHA = (2 * DEPTH) ** 0.25
DEEPNORM_BETA = (8 * DEPTH) ** -0.25

kernel_name = "hybrid_gmlp_natten_diffattn_encoder"


def layer_norm(x, g, b):
    xf = x.astype(jnp.float32)
    mu = jnp.mean(xf, axis=-1, keepdims=True)
    var = jnp.mean(jnp.square(xf - mu), axis=-1, keepdims=True)
    return ((xf - mu) * lax.rsqrt(var + LN_EPS) * g + b).astype(x.dtype)


def rms_norm(x, g):
    xf = x.astype(jnp.float32)
    ms = jnp.mean(jnp.square(xf), axis=-1, keepdims=True)
    return (xf * lax.rsqrt(ms + LN_EPS) * g).astype(x.dtype)


def lambda_init(layer):
    return 0.8 - 0.6 * math.exp(-0.3 * layer)


def chunked_spatial_gating(z, ln_g, ln_b, w_s, b_s):
    B, T, _ = z.shape
    u, v = jnp.split(z, 2, axis=-1)
    v = layer_norm(v, ln_g, ln_b)
    nc = T // CHUNK
    v = v.reshape(B, nc, CHUNK, A_GROUPS, A_GROUP_DIM)
    mixed = jnp.einsum('gts,bnsgc->bntgc', w_s, v) + b_s.T[None, None, :, :, None]
    return u * mixed.reshape(B, T, A_WIDTH)


def neighbourhood_attention(q, k, v, rpb):
    B, T, H, dh = q.shape
    rows = T // GRID_W
    kh = min(NA_KH_MAX, rows)
    kw = NA_KW
    qg = q.reshape(B, rows, GRID_W, H, dh) * (dh ** -0.5)
    kg = k.reshape(B, rows, GRID_W, H, dh)
    vg = v.reshape(B, rows, GRID_W, H, dh)
    cols = jnp.arange(GRID_W)
    c_start = jnp.clip(cols - kw // 2, 0, GRID_W - kw)
    col_idx = c_start[:, None] + jnp.arange(kw)[None, :]
    col_off = col_idx - cols[:, None] + (NA_KW - 1)

    def one_row(r):
        r_start = jnp.clip(r - kh // 2, 0, rows - kh)
        k_rows = lax.dynamic_slice_in_dim(kg, r_start, kh, axis=1)
        v_rows = lax.dynamic_slice_in_dim(vg, r_start, kh, axis=1)
        k_win = k_rows[:, :, col_idx]
        v_win = v_rows[:, :, col_idx]
        q_row = lax.dynamic_index_in_dim(qg, r, axis=1, keepdims=False)
        s = jnp.einsum('bchd,bicjhd->bhcij', q_row, k_win).astype(jnp.float32)
        row_off = r_start + jnp.arange(kh) - r + (NA_KH_MAX - 1)
        bias = rpb[:, row_off][:, :, col_off]
        s = s + jnp.transpose(bias, (0, 2, 1, 3))[None]
        p = jax.nn.softmax(s.reshape(B, H, GRID_W, kh * kw), axis=-1)
        p = p.reshape(B, H, GRID_W, kh, kw).astype(v.dtype)
        return jnp.einsum('bhcij,bicjhd->bchd', p, v_win)

    out = lax.map(one_row, jnp.arange(rows))
    return jnp.transpose(out, (1, 0, 2, 3, 4)).reshape(B, T, H * dh)


def mixer_gmlp_natten(x, w_in, w_out, gate_ln_g, gate_ln_b, w_spatial, b_spatial, na_rpb):
    B, T, _ = x.shape
    h = x @ w_in
    z_a, q, k, v = jnp.split(h, [2 * A_WIDTH, 2 * A_WIDTH + NA_WIDTH, 2 * A_WIDTH + 2 * NA_WIDTH], axis=-1)
    out_a = chunked_spatial_gating(jax.nn.gelu(z_a), gate_ln_g, gate_ln_b, w_spatial, b_spatial)
    shp = (B, T, NA_HEADS, NA_HEAD_DIM)
    out_b = neighbourhood_attention(q.reshape(shp), k.reshape(shp), v.reshape(shp), na_rpb)
    return jnp.concatenate([out_a, out_b], axis=-1) @ w_out


def differential_attention(q, k, v, lam, lam_init):
    B, T, H, _, dh = q.shape
    nb = T // Q_BLOCK
    slopes = jnp.exp2(-8.0 * jnp.arange(1, H + 1, dtype=jnp.float32) / H)
    qb = jnp.transpose((q * (dh ** -0.5)).reshape(B, nb, Q_BLOCK, H, 2, dh), (1, 0, 2, 3, 4, 5))
    k_pos = jnp.arange(T)

    def one_block(args):
        q_blk, i = args
        q_pos = i * Q_BLOCK + jnp.arange(Q_BLOCK)
        dist = jnp.abs(q_pos[:, None] - k_pos[None, :]).astype(jnp.float32)
        bias = -slopes[:, None, None] * dist
        s = jnp.einsum('bqhnd,bkhnd->bhnqk', q_blk, k).astype(jnp.float32) + bias[None, :, None]
        p = jax.nn.softmax(s, axis=-1)
        p = (p[:, :, 0] - lam * p[:, :, 1]).astype(v.dtype)
        return jnp.einsum('bhqk,bkhe->bqhe', p, v)

    o = lax.map(one_block, (qb, jnp.arange(nb)))
    return jnp.transpose(o, (1, 0, 2, 3, 4)).reshape(B, T, H, 2 * dh)


def mixer_diff(x, w_in, w_out, lambda_q1, lambda_k1, lambda_q2, lambda_k2, subln_g, lam_init):
    B, T, _ = x.shape
    h = x @ w_in
    q, k, v = jnp.split(h, 3, axis=-1)
    q = q.reshape(B, T, DIFF_HEADS, 2, DIFF_HEAD_DIM)
    k = k.reshape(B, T, DIFF_HEADS, 2, DIFF_HEAD_DIM)
    v = v.reshape(B, T, DIFF_HEADS, 2 * DIFF_HEAD_DIM)
    lam1 = jnp.exp(jnp.sum((lambda_q1 * lambda_k1).astype(jnp.float32)))
    lam2 = jnp.exp(jnp.sum((lambda_q2 * lambda_k2).astype(jnp.float32)))
    lam = lam1 - lam2 + lam_init
    o = differential_attention(q, k, v, lam, lam_init)
    o = rms_norm(o, subln_g) * (1.0 - lam_init)
    return o.reshape(B, T, DIFF_WIDTH) @ w_out


def squared_relu_mlp(x, w1, w2):
    return jnp.square(jax.nn.relu(x @ w1)) @ w2


def setup_inputs(seed: int = 0) -> dict:
    key = jax.random.key(seed)
    ks = jax.random.split(key, 32)
    f32 = jnp.float32
    d = D_MODEL

    def nrm(k, shape, scale):
        return jax.random.normal(k, shape, f32) * scale

    def gain(k, n):
        return 1.0 + 0.02 * jax.random.normal(k, (n,), f32)

    w_in0 = nrm(ks[2], (d, 2 * A_WIDTH + 3 * NA_WIDTH), d ** -0.5)
    w_in0 = w_in0.at[:, 2 * A_WIDTH + 2 * NA_WIDTH:].multiply(DEEPNORM_BETA)
    w_in1 = nrm(ks[15], (d, 3 * DIFF_WIDTH), d ** -0.5)
    w_in1 = w_in1.at[:, 2 * DIFF_WIDTH:].multiply(DEEPNORM_BETA)
    return {
        "x_prompt": jax.random.normal(ks[0], (BATCH, SEQ, d), f32),
        "x_sample": jax.random.normal(ks[1], (DEC_BATCH, DEC_SEQ, d), f32),
        "l0_w_in": w_in0,
        "l0_w_out": nrm(ks[3], (A_WIDTH + NA_WIDTH, d), (A_WIDTH + NA_WIDTH) ** -0.5 * DEEPNORM_BETA),
        "l0_gate_ln_g": gain(ks[4], A_WIDTH),
        "l0_gate_ln_b": nrm(ks[5], (A_WIDTH,), 0.02),
        "l0_w_spatial": nrm(ks[6], (A_GROUPS, CHUNK, CHUNK), CHUNK ** -0.5),
        "l0_b_spatial": 1.0 + nrm(ks[7], (A_GROUPS, CHUNK), 0.02),
        "l0_na_rpb": nrm(ks[8], (NA_HEADS, 2 * NA_KH_MAX - 1, 2 * NA_KW - 1), 0.02),
        "l0_ln1_g": gain(ks[9], d),
        "l0_ln1_b": nrm(ks[10], (d,), 0.02),
        "l0_w_ff1": nrm(ks[11], (d, D_FF), d ** -0.5 * DEEPNORM_BETA),
        "l0_w_ff2": nrm(ks[12], (D_FF, d), D_FF ** -0.5 * DEEPNORM_BETA),
        "l0_ln2_g": gain(ks[13], d),
        "l0_ln2_b": nrm(ks[14], (d,), 0.02),
        "l1_w_in": w_in1,
        "l1_w_out": nrm(ks[16], (DIFF_WIDTH, d), DIFF_WIDTH ** -0.5 * DEEPNORM_BETA),
        "l1_lambda_q1": nrm(ks[17], (DIFF_HEAD_DIM,), 0.1),
        "l1_lambda_k1": nrm(ks[18], (DIFF_HEAD_DIM,), 0.1),
        "l1_lambda_q2": nrm(ks[19], (DIFF_HEAD_DIM,), 0.1),
        "l1_lambda_k2": nrm(ks[20], (DIFF_HEAD_DIM,), 0.1),
        "l1_subln_g": gain(ks[21], 2 * DIFF_HEAD_DIM),
        "l1_ln1_g": gain(ks[22], d),
        "l1_ln1_b": nrm(ks[23], (d,), 0.02),
        "l1_w_ff1": nrm(ks[24], (d, D_FF), d ** -0.5 * DEEPNORM_BETA),
        "l1_w_ff2": nrm(ks[25], (D_FF, d), D_FF ** -0.5 * DEEPNORM_BETA),
        "l1_ln2_g": gain(ks[26], d),
        "l1_ln2_b": nrm(ks[27], (d,), 0.02),
    }


def reference(x_prompt, x_sample, l0_w_in, l0_w_out, l0_gate_ln_g, l0_gate_ln_b, l0_w_spatial,
              l0_b_spatial, l0_na_rpb, l0_ln1_g, l0_ln1_b, l0_w_ff1, l0_w_ff2, l0_ln2_g, l0_ln2_b,
              l1_w_in, l1_w_out, l1_lambda_q1, l1_lambda_k1, l1_lambda_q2, l1_lambda_k2, l1_subln_g,
              l1_ln1_g, l1_ln1_b, l1_w_ff1, l1_w_ff2, l1_ln2_g, l1_ln2_b):
    ffn = [(l0_w_ff1, l0_w_ff2), (l1_w_ff1, l1_w_ff2)]
    ln1 = [(l0_ln1_g, l0_ln1_b), (l1_ln1_g, l1_ln1_b)]
    ln2 = [(l0_ln2_g, l0_ln2_b), (l1_ln2_g, l1_ln2_b)]

    def trunk(x):
        for layer in range(DEPTH):
            if layer % 2 == 0:
                mix = mixer_gmlp_natten(x, l0_w_in, l0_w_out, l0_gate_ln_g, l0_gate_ln_b,
                                        l0_w_spatial, l0_b_spatial, l0_na_rpb)
            else:
                mix = mixer_diff(x, l1_w_in, l1_w_out, l1_lambda_q1, l1_lambda_k1, l1_lambda_q2,
                                 l1_lambda_k2, l1_subln_g, lambda_init(layer))
            x = layer_norm(DEEPNORM_ALPHA * x + mix, *ln1[layer])
            x = layer_norm(DEEPNORM_ALPHA * x + squared_relu_mlp(x, *ffn[layer]), *ln2[layer])
        return x

    y_prompt = trunk(x_prompt)
    y_sample = trunk(x_sample)
    return (y_prompt, y_sample)
```

```python
import functools
import math

import jax
import jax.numpy as jnp
from jax import lax
from jax.experimental import pallas as pl
from jax.experimental.pallas import tpu as pltpu

F32 = jnp.float32
BF16 = jnp.bfloat16

D_MODEL = 1024
DEPTH = 2
GRID_W = 64
CHUNK = 128
A_WIDTH = 512
A_GROUPS = 4
NA_HEADS = 8
NA_HEAD_DIM = 64
NA_WIDTH = 512
NA_KH = 8
NA_KW = 16
DIFF_HEADS = 8
DIFF_HEAD_DIM = 64
D_FF = 4 * D_MODEL
LN_EPS = 1e-5
ALPHA = (2 * DEPTH) ** 0.25
LAM_INIT_1 = 0.8 - 0.6 * math.exp(-0.3 * 1)
NEG = -1e30

LANES = 128
VMEM_LIMIT = 56 * 1024 * 1024

TM_PROJ0 = 512
TM_POST = 512
TK_DIFF = 512
TQ_DIFF = 256
NA_ROWS_PER_STEP = 8
FF_CHUNK = 1024


def _const_spec(shape):
    nd = len(shape)
    return pl.BlockSpec(shape, lambda *_: (0,) * nd, pipeline_mode=pl.Buffered(1))


def _layer_norm(y, g, b):
    mu = jnp.mean(y, axis=-1, keepdims=True)
    yc = y - mu
    var = jnp.mean(yc * yc, axis=-1, keepdims=True)
    return yc * lax.rsqrt(var + LN_EPS) * g + b


def _dot(a, b):
    return jnp.dot(a, b, preferred_element_type=F32)


def _dot_nt(a, b):
    return lax.dot_general(a, b, (((1,), (1,)), ((), ())), preferred_element_type=F32)


def _proj0_kernel(x_ref, w_ref, lng_ref, lnb_ref, ws_ref, bs_ref, oa_ref, q_ref, k_ref, v_ref):
    tm = x_ref.shape[0]
    xb = x_ref[...].astype(BF16)

    def mm(lo, hi):
        return _dot(xb, w_ref[:, lo:hi])

    u = jax.nn.gelu(mm(0, A_WIDTH))
    v = jax.nn.gelu(mm(A_WIDTH, 2 * A_WIDTH))
    vn = _layer_norm(v, lng_ref[...], lnb_ref[...]).astype(BF16)
    for c in range(tm // CHUNK):
        rs = slice(c * CHUNK, (c + 1) * CHUNK)
        for g in range(A_GROUPS):
            cs = slice(g * LANES, (g + 1) * LANES)
            mixed = _dot(ws_ref[g], vn[rs, cs]) + bs_ref[g]
            oa_ref[rs, cs] = (u[rs, cs] * mixed).astype(BF16)
    base = 2 * A_WIDTH
    q_ref[...] = (mm(base, base + NA_WIDTH) * (NA_HEAD_DIM ** -0.5)).astype(BF16)
    k_ref[...] = mm(base + NA_WIDTH, base + 2 * NA_WIDTH).astype(BF16)
    v_ref[...] = mm(base + 2 * NA_WIDTH, base + 3 * NA_WIDTH).astype(BF16)


def _proj0(x2d, w_in, ln_g, ln_b, w_s, b_s):
    n = x2d.shape[0]
    tm = TM_PROJ0
    row = lambda width: pl.BlockSpec((tm, width), lambda i: (i, 0))
    out = jax.ShapeDtypeStruct((n, A_WIDTH), BF16)
    return pl.pallas_call(
        _proj0_kernel,
        grid=(n // tm,),
        in_specs=[row(D_MODEL), _const_spec(w_in.shape), _const_spec(ln_g.shape), _const_spec(ln_b.shape),
                  _const_spec(w_s.shape), _const_spec(b_s.shape)],
        out_specs=[row(A_WIDTH)] * 4,
        out_shape=[out] * 4,
        compiler_params=pltpu.CompilerParams(dimension_semantics=("arbitrary",), vmem_limit_bytes=VMEM_LIMIT),
        name="proj0_sgu",
    )(x2d, w_in, ln_g, ln_b, w_s, b_s)


def _natten_kernel(q_ref, k_ref, v_ref, bias_ref, o_ref, *, rows, rb):
    i = pl.program_id(1)
    lane = lax.broadcasted_iota(jnp.int32, (GRID_W, LANES), 1)
    lo = lane < NA_HEAD_DIM
    nkeys = NA_KH * GRID_W
    for rr in range(rb):
        r = i * rb + rr
        r_start = jnp.clip(r - NA_KH // 2, 0, rows - NA_KH)
        d = r - r_start
        kbase = pl.multiple_of(r_start * GRID_W, GRID_W)
        qrows = slice(rr * GRID_W, (rr + 1) * GRID_W)
        for hp in range(NA_HEADS // 2):
            cs = slice(hp * LANES, (hp + 1) * LANES)
            qp = q_ref[qrows, cs]
            z = jnp.zeros_like(qp)
            qq = jnp.concatenate([jnp.where(lo, qp, z), jnp.where(lo, z, qp)], axis=0)
            kk = k_ref[pl.ds(kbase, nkeys), cs]
            vv = v_ref[pl.ds(kbase, nkeys), cs]
            s = _dot_nt(qq, kk) + bias_ref[d, hp]
            m = jnp.max(s, axis=-1, keepdims=True)
            e = jnp.exp(s - m)
            l = jnp.sum(e, axis=-1, keepdims=True)
            pv = _dot(e.astype(BF16), vv) / l
            o_ref[qrows, cs] = jnp.where(lo, pv[:GRID_W], pv[GRID_W:]).astype(BF16)


def _natten(q, k, v, bias, batch, seq):
    rows = seq // GRID_W
    rb = NA_ROWS_PER_STEP
    nblk = rows // rb
    kern = functools.partial(_natten_kernel, rows=rows, rb=rb)
    qspec = pl.BlockSpec((rb * GRID_W, NA_WIDTH), lambda b, i: (b * nblk + i, 0))
    kvspec = pl.BlockSpec((seq, NA_WIDTH), lambda b, i: (b, 0))
    return pl.pallas_call(
        kern,
        grid=(batch, nblk),
        in_specs=[qspec, kvspec, kvspec, _const_spec(bias.shape)],
        out_specs=qspec,
        out_shape=jax.ShapeDtypeStruct(q.shape, BF16),
        compiler_params=pltpu.CompilerParams(dimension_semantics=("arbitrary", "arbitrary"),
                                             vmem_limit_bytes=VMEM_LIMIT),
        name="natten",
    )(q, k, v, bias)


def _na_bias_table(rpb):
    c = jnp.arange(GRID_W)
    kc = jnp.arange(GRID_W)
    c_start = jnp.clip(c - NA_KW // 2, 0, GRID_W - NA_KW)
    valid = (kc[None, :] >= c_start[:, None]) & (kc[None, :] < c_start[:, None] + NA_KW)
    coff = jnp.clip(kc[None, :] - c[:, None] + (NA_KW - 1), 0, 2 * NA_KW - 2)
    d = jnp.arange(NA_KH)
    i = jnp.arange(NA_KH)
    roff = i[None, :] - d[:, None] + (NA_KH - 1)
    t = rpb[:, roff]
    t = t[:, :, :, coff]
    t = jnp.where(valid[None, None, None], t, NEG)
    t = jnp.transpose(t, (1, 0, 3, 2, 4))
    return t.reshape(NA_KH, NA_HEADS // 2, 2 * GRID_W, NA_KH * GRID_W).astype(F32)


def _post_kernel(*refs, n_in):
    a_refs = refs[:n_in]
    wo_refs = refs[n_in:2 * n_in]
    x_ref, g1_ref, b1_ref, w1_ref, w2_ref, g2_ref, b2_ref, o_ref = refs[2 * n_in:]
    mix = _dot(a_refs[0][...], wo_refs[0][...])
    for a_ref, wo_ref in zip(a_refs[1:], wo_refs[1:]):
        mix = mix + _dot(a_ref[...], wo_ref[...])
    x1 = _layer_norm(ALPHA * x_ref[...] + mix, g1_ref[...], b1_ref[...])
    x1b = x1.astype(BF16)
    acc = None
    for j in range(D_FF // FF_CHUNK):
        fs = slice(j * FF_CHUNK, (j + 1) * FF_CHUNK)
        hid = jnp.square(jnp.maximum(_dot(x1b, w1_ref[:, fs]), 0.0)).astype(BF16)
        part = _dot(hid, w2_ref[fs, :])
        acc = part if acc is None else acc + part
    o_ref[...] = _layer_norm(ALPHA * x1 + acc, g2_ref[...], b2_ref[...])


def _post(acts, w_outs, x2d, g1, b1, w1, w2, g2, b2):
    n = x2d.shape[0]
    tm = TM_POST
    n_in = len(acts)
    row = lambda width: pl.BlockSpec((tm, width), lambda i: (i, 0))
    in_specs = ([row(a.shape[1]) for a in acts] + [_const_spec(w.shape) for w in w_outs]
                + [row(D_MODEL)] + [_const_spec(p.shape) for p in (g1, b1, w1, w2, g2, b2)])
    return pl.pallas_call(
        functools.partial(_post_kernel, n_in=n_in),
        grid=(n // tm,),
        in_specs=in_specs,
        out_specs=row(D_MODEL),
        out_shape=jax.ShapeDtypeStruct((n, D_MODEL), F32),
        compiler_params=pltpu.CompilerParams(dimension_semantics=("arbitrary",), vmem_limit_bytes=VMEM_LIMIT),
        name="post",
    )(*acts, *w_outs, x2d, g1, b1, w1, w2, g2, b2)


def _proj1_kernel(x_ref, wq_ref, wk_ref, wvt_ref, q_ref, k_ref, vt_ref):
    xb = x_ref[...].astype(BF16)
    q_ref[...] = (_dot(xb, wq_ref[...]) * (DIFF_HEAD_DIM ** -0.5)).astype(BF16)
    k_ref[...] = _dot(xb, wk_ref[...]).astype(BF16)
    vt_ref[0] = _dot_nt(wvt_ref[...], xb).astype(BF16)


def _proj1(x2d, wq, wk, wvt):
    n = x2d.shape[0]
    tm = TK_DIFF
    row = pl.BlockSpec((tm, D_MODEL), lambda i: (i, 0))
    return pl.pallas_call(
        _proj1_kernel,
        grid=(n // tm,),
        in_specs=[row, _const_spec(wq.shape), _const_spec(wk.shape), _const_spec(wvt.shape)],
        out_specs=[row, row, pl.BlockSpec((1, D_MODEL, tm), lambda i: (i, 0, 0))],
        out_shape=[jax.ShapeDtypeStruct((n, D_MODEL), BF16), jax.ShapeDtypeStruct((n, D_MODEL), BF16),
                   jax.ShapeDtypeStruct((n // tm, D_MODEL, tm), BF16)],
        compiler_params=pltpu.CompilerParams(dimension_semantics=("arbitrary",), vmem_limit_bytes=VMEM_LIMIT),
        name="proj1",
    )(x2d, wq, wk, wvt)


def _diff_kernel(lam_ref, slope_ref, q_ref, k_ref, vt_ref, g_ref, o_ref, *, tq, tk, nk):
    h = pl.program_id(1)
    i = pl.program_id(2)
    slope = slope_ref[h]
    lam = lam_ref[0]
    q = q_ref[...]
    lane = lax.broadcasted_iota(jnp.int32, q.shape, 1)
    lo = lane < DIFF_HEAD_DIM
    z = jnp.zeros_like(q)
    qq = jnp.concatenate([jnp.where(lo, q, z), jnp.where(lo, z, q)], axis=0)
    qio = lax.broadcasted_iota(jnp.int32, (1, 2 * tq), 1)
    qpos = (i * tq + jnp.where(qio >= tq, qio - tq, qio)).astype(F32)
    kio = lax.broadcasted_iota(jnp.int32, (tk, 1), 0).astype(F32)
    m = jnp.full((1, 2 * tq), -jnp.inf, F32)
    l = jnp.zeros((1, 2 * tq), F32)
    acc = jnp.zeros((2 * DIFF_HEAD_DIM, 2 * tq), F32)
    for j in range(nk):
        s = _dot_nt(k_ref[j * tk:(j + 1) * tk, :], qq)
        s = s - slope * jnp.abs((kio + float(j * tk)) - qpos)
        m_new = jnp.maximum(m, jnp.max(s, axis=0, keepdims=True))
        alpha = jnp.exp(m - m_new)
        p = jnp.exp(s - m_new)
        l = alpha * l + jnp.sum(p, axis=0, keepdims=True)
        acc = alpha * acc + _dot(vt_ref[j], p.astype(BF16))
        m = m_new
    o = acc[:, :tq] / l[:, :tq] - lam * (acc[:, tq:] / l[:, tq:])
    ms = jnp.mean(o * o, axis=0, keepdims=True)
    o = o * lax.rsqrt(ms + LN_EPS) * g_ref[...] * (1.0 - LAM_INIT_1)
    o_ref[...] = o.T.astype(BF16)


def _diffattn(lam, slopes, q, k, vt, g_col, batch, seq):
    tq, tk = TQ_DIFF, TK_DIFF
    nq, nk = seq // tq, seq // tk
    hd = 2 * DIFF_HEAD_DIM
    smem = pl.BlockSpec(memory_space=pltpu.SMEM)
    return pl.pallas_call(
        functools.partial(_diff_kernel, tq=tq, tk=tk, nk=nk),
        grid=(batch, DIFF_HEADS, nq),
        in_specs=[smem, smem,
                  pl.BlockSpec((tq, hd), lambda b, h, i: (b * nq + i, h)),
                  pl.BlockSpec((seq, hd), lambda b, h, i: (b, h)),
                  pl.BlockSpec((nk, hd, tk), lambda b, h, i: (b, h, 0)),
                  pl.BlockSpec(g_col.shape, lambda b, h, i: (0, 0))],
        out_specs=pl.BlockSpec((tq, hd), lambda b, h, i: (b * nq + i, h)),
        out_shape=jax.ShapeDtypeStruct(q.shape, BF16),
        compiler_params=pltpu.CompilerParams(dimension_semantics=("arbitrary",) * 3,
                                             vmem_limit_bytes=VMEM_LIMIT),
        name="diffattn",
    )(lam, slopes, q, k, vt, g_col)


def kernel(x_prompt, x_sample, l0_w_in, l0_w_out, l0_gate_ln_g, l0_gate_ln_b, l0_w_spatial, l0_b_spatial, l0_na_rpb, l0_ln1_g, l0_ln1_b, l0_w_ff1, l0_w_ff2, l0_ln2_g, l0_ln2_b, l1_w_in, l1_w_out, l1_lambda_q1, l1_lambda_k1, l1_lambda_q2, l1_lambda_k2, l1_subln_g, l1_ln1_g, l1_ln1_b, l1_w_ff1, l1_w_ff2, l1_ln2_g, l1_ln2_b):
    row = lambda p: p.reshape(1, -1).astype(F32)
    w_in0 = l0_w_in.astype(BF16)
    w_out0a = l0_w_out[:A_WIDTH].astype(BF16)
    w_out0b = l0_w_out[A_WIDTH:].astype(BF16)
    w_s = l0_w_spatial.astype(BF16)
    b_s = jnp.broadcast_to(l0_b_spatial.astype(F32)[:, :, None], (A_GROUPS, CHUNK, LANES))
    na_bias = _na_bias_table(l0_na_rpb.astype(F32))
    wq1 = l1_w_in[:, :D_MODEL].astype(BF16)
    wk1 = l1_w_in[:, D_MODEL:2 * D_MODEL].astype(BF16)
    wvt1 = l1_w_in[:, 2 * D_MODEL:].T.astype(BF16)
    w_out1 = l1_w_out.astype(BF16)
    lam1 = jnp.exp(jnp.sum((l1_lambda_q1 * l1_lambda_k1).astype(F32)))
    lam2 = jnp.exp(jnp.sum((l1_lambda_q2 * l1_lambda_k2).astype(F32)))
    lam = (lam1 - lam2 + LAM_INIT_1).reshape(1).astype(F32)
    slopes = jnp.exp2(-8.0 * jnp.arange(1, DIFF_HEADS + 1, dtype=F32) / DIFF_HEADS)
    g_col = l1_subln_g.astype(F32).reshape(-1, 1)
    ff0 = (l0_w_ff1.astype(BF16), l0_w_ff2.astype(BF16))
    ff1 = (l1_w_ff1.astype(BF16), l1_w_ff2.astype(BF16))

    def trunk(x):
        batch, seq, _ = x.shape
        x2d = x.reshape(batch * seq, D_MODEL)
        out_a, q, k, v = _proj0(x2d, w_in0, row(l0_gate_ln_g), row(l0_gate_ln_b), w_s, b_s)
        out_b = _natten(q, k, v, na_bias, batch, seq)
        x2d = _post([out_a, out_b], [w_out0a, w_out0b], x2d, row(l0_ln1_g), row(l0_ln1_b),
                    ff0[0], ff0[1], row(l0_ln2_g), row(l0_ln2_b))
        q, k, vt = _proj1(x2d, wq1, wk1, wvt1)
        o = _diffattn(lam, slopes, q, k, vt, g_col, batch, seq)
        x2d = _post([o], [w_out1], x2d, row(l1_ln1_g), row(l1_ln1_b),
                    ff1[0], ff1[1], row(l1_ln2_g), row(l1_ln2_b))
        return x2d.reshape(batch, seq, D_MODEL)

    return (trunk(x_prompt), trunk(x_sample))
```

```python
import functools
import math

import jax
import jax.numpy as jnp
import numpy as np
from jax import lax
from jax.experimental import pallas as pl
from jax.experimental.pallas import tpu as pltpu

F32 = jnp.float32
BF16 = jnp.bfloat16

D_MODEL = 1024
DEPTH = 2
GRID_W = 64
CHUNK = 128
A_WIDTH = 512
A_GROUPS = 4
NA_HEADS = 8
NA_HEAD_DIM = 64
NA_WIDTH = 512
NA_KH = 8
NA_KW = 16
DIFF_HEADS = 8
DIFF_HEAD_DIM = 64
D_FF = 4 * D_MODEL
LN_EPS = 1e-5
ALPHA = (2 * DEPTH) ** 0.25
LAM_INIT_1 = 0.8 - 0.6 * math.exp(-0.3 * 1)
NEG = -1e30

LANES = 128
VMEM_LIMIT = 56 * 1024 * 1024

TM_PROJ0 = 512
TM_POST = 512
TK_DIFF = 512
NA_ROWS_PER_STEP = 8
FF_CHUNK = 1024


def _const_spec(shape):
    nd = len(shape)
    return pl.BlockSpec(shape, lambda *_: (0,) * nd, pipeline_mode=pl.Buffered(1))


def _layer_norm(y, g, b):
    mu = jnp.mean(y, axis=-1, keepdims=True)
    yc = y - mu
    var = jnp.mean(yc * yc, axis=-1, keepdims=True)
    return yc * lax.rsqrt(var + LN_EPS) * g + b


def _dot(a, b):
    return jnp.dot(a, b, preferred_element_type=F32)


def _dot_nt(a, b):
    return lax.dot_general(a, b, (((1,), (1,)), ((), ())), preferred_element_type=F32)


def _proj0_kernel(x_ref, w_ref, lng_ref, lnb_ref, ws_ref, bs_ref, oa_ref, q_ref, k_ref, v_ref):
    tm = x_ref.shape[0]
    xb = x_ref[...].astype(BF16)

    def mm(lo, hi):
        return _dot(xb, w_ref[:, lo:hi])

    u = jax.nn.gelu(mm(0, A_WIDTH))
    v = jax.nn.gelu(mm(A_WIDTH, 2 * A_WIDTH))
    vn = _layer_norm(v, lng_ref[...], lnb_ref[...]).astype(BF16)
    for c in range(tm // CHUNK):
        rs = slice(c * CHUNK, (c + 1) * CHUNK)
        for g in range(A_GROUPS):
            cs = slice(g * LANES, (g + 1) * LANES)
            mixed = _dot(ws_ref[g], vn[rs, cs]) + bs_ref[g]
            oa_ref[rs, cs] = (u[rs, cs] * mixed).astype(BF16)
    base = 2 * A_WIDTH
    q_ref[...] = (mm(base, base + NA_WIDTH) * (NA_HEAD_DIM ** -0.5)).astype(BF16)
    k_ref[...] = mm(base + NA_WIDTH, base + 2 * NA_WIDTH).astype(BF16)
    v_ref[...] = mm(base + 2 * NA_WIDTH, base + 3 * NA_WIDTH).astype(BF16)


def _proj0(x2d, w_in, ln_g, ln_b, w_s, b_s):
    n = x2d.shape[0]
    tm = TM_PROJ0
    row = lambda width: pl.BlockSpec((tm, width), lambda i: (i, 0))
    out = jax.ShapeDtypeStruct((n, A_WIDTH), BF16)
    return pl.pallas_call(
        _proj0_kernel,
        grid=(n // tm,),
        in_specs=[row(D_MODEL), _const_spec(w_in.shape), _const_spec(ln_g.shape), _const_spec(ln_b.shape),
                  _const_spec(w_s.shape), _const_spec(b_s.shape)],
        out_specs=[row(A_WIDTH)] * 4,
        out_shape=[out] * 4,
        compiler_params=pltpu.CompilerParams(dimension_semantics=("arbitrary",), vmem_limit_bytes=VMEM_LIMIT),
        name="proj0_sgu",
    )(x2d, w_in, ln_g, ln_b, w_s, b_s)


def _natten_kernel(q_ref, k_ref, v_ref, bias_ref, o_ref, *, rows, rb):
    i = pl.program_id(1)
    lane = lax.broadcasted_iota(jnp.int32, (GRID_W, LANES), 1)
    lo = lane < NA_HEAD_DIM
    nkeys = NA_KH * GRID_W
    ones = jnp.ones((nkeys, LANES), BF16)
    for rr in range(rb):
        r = i * rb + rr
        r_start = jnp.clip(r - NA_KH // 2, 0, rows - NA_KH)
        d = r - r_start
        kbase = pl.multiple_of(r_start * GRID_W, GRID_W)
        qrows = slice(rr * GRID_W, (rr + 1) * GRID_W)
        for hp in range(NA_HEADS // 2):
            cs = slice(hp * LANES, (hp + 1) * LANES)
            qp = q_ref[qrows, cs]
            z = jnp.zeros_like(qp)
            qq = jnp.concatenate([jnp.where(lo, qp, z), jnp.where(lo, z, qp)], axis=0)
            kk = k_ref[pl.ds(kbase, nkeys), cs]
            vv = jnp.concatenate([v_ref[pl.ds(kbase, nkeys), cs], ones], axis=1)
            s = _dot_nt(qq, kk) + bias_ref[d, hp]
            m = jnp.max(s, axis=-1, keepdims=True)
            pv = _dot(jnp.exp(s - m).astype(BF16), vv)
            pv = pv[:, :LANES] / pv[:, LANES:]
            o_ref[qrows, cs] = jnp.where(lo, pv[:GRID_W], pv[GRID_W:]).astype(BF16)


def _natten(q, k, v, bias, batch, seq):
    rows = seq // GRID_W
    rb = NA_ROWS_PER_STEP
    nblk = rows // rb
    kern = functools.partial(_natten_kernel, rows=rows, rb=rb)
    qspec = pl.BlockSpec((rb * GRID_W, NA_WIDTH), lambda b, i: (b * nblk + i, 0))
    kvspec = pl.BlockSpec((seq, NA_WIDTH), lambda b, i: (b, 0))
    return pl.pallas_call(
        kern,
        grid=(batch, nblk),
        in_specs=[qspec, kvspec, kvspec, _const_spec(bias.shape)],
        out_specs=qspec,
        out_shape=jax.ShapeDtypeStruct(q.shape, BF16),
        compiler_params=pltpu.CompilerParams(dimension_semantics=("arbitrary", "arbitrary"),
                                             vmem_limit_bytes=VMEM_LIMIT),
        name="natten",
    )(q, k, v, bias)


def _na_bias_table(rpb):
    c = np.arange(GRID_W)
    kc = np.arange(GRID_W)
    c_start = np.clip(c - NA_KW // 2, 0, GRID_W - NA_KW)
    valid = (kc[None, :] >= c_start[:, None]) & (kc[None, :] < c_start[:, None] + NA_KW)
    coff = kc[None, :] - c[:, None] + (NA_KW - 1)
    sel = (valid[None] & (coff[None] == np.arange(2 * NA_KW - 1)[:, None, None])).astype(np.float32)
    t = jnp.einsum("hro,ock->hrck", rpb, jnp.asarray(sel), precision=lax.Precision.HIGHEST)
    t = t + jnp.asarray(np.where(valid, 0.0, NEG).astype(np.float32))[None, None]
    t = jnp.stack([t[:, NA_KH - 1 - d:2 * NA_KH - 1 - d] for d in range(NA_KH)])
    t = jnp.transpose(t, (0, 1, 3, 2, 4))
    return t.reshape(NA_KH, NA_HEADS // 2, 2 * GRID_W, NA_KH * GRID_W).astype(F32)


def _post_kernel(*refs, n_in):
    a_refs = refs[:n_in]
    wo_refs = refs[n_in:2 * n_in]
    x_ref, g1_ref, b1_ref, w1_ref, w2_ref, g2_ref, b2_ref, o_ref = refs[2 * n_in:]
    mix = _dot(a_refs[0][...], wo_refs[0][...])
    for a_ref, wo_ref in zip(a_refs[1:], wo_refs[1:]):
        mix = mix + _dot(a_ref[...], wo_ref[...])
    x1 = _layer_norm(ALPHA * x_ref[...] + mix, g1_ref[...], b1_ref[...])
    x1b = x1.astype(BF16)
    acc = None
    for j in range(D_FF // FF_CHUNK):
        fs = slice(j * FF_CHUNK, (j + 1) * FF_CHUNK)
        hid = jnp.square(jnp.maximum(_dot(x1b, w1_ref[:, fs]), 0.0)).astype(BF16)
        part = _dot(hid, w2_ref[fs, :])
        acc = part if acc is None else acc + part
    o_ref[...] = _layer_norm(ALPHA * x1 + acc, g2_ref[...], b2_ref[...])


def _post(acts, w_outs, x2d, g1, b1, w1, w2, g2, b2):
    n = x2d.shape[0]
    tm = TM_POST
    n_in = len(acts)
    row = lambda width: pl.BlockSpec((tm, width), lambda i: (i, 0))
    in_specs = ([row(a.shape[1]) for a in acts] + [_const_spec(w.shape) for w in w_outs]
                + [row(D_MODEL)] + [_const_spec(p.shape) for p in (g1, b1, w1, w2, g2, b2)])
    return pl.pallas_call(
        functools.partial(_post_kernel, n_in=n_in),
        grid=(n // tm,),
        in_specs=in_specs,
        out_specs=row(D_MODEL),
        out_shape=jax.ShapeDtypeStruct((n, D_MODEL), F32),
        compiler_params=pltpu.CompilerParams(dimension_semantics=("arbitrary",), vmem_limit_bytes=VMEM_LIMIT),
        name="post",
    )(*acts, *w_outs, x2d, g1, b1, w1, w2, g2, b2)


def _proj1_kernel(x_ref, wq_ref, wk_ref, wvt_ref, q_ref, k_ref, vt_ref):
    xb = x_ref[...].astype(BF16)
    q_ref[...] = (_dot(xb, wq_ref[...]) * (DIFF_HEAD_DIM ** -0.5)).astype(BF16)
    k_ref[...] = _dot(xb, wk_ref[...]).astype(BF16)
    vt_ref[0] = _dot_nt(wvt_ref[...], xb).astype(BF16)


def _proj1(x2d, wq, wk, wvt):
    n = x2d.shape[0]
    tm = TK_DIFF
    row = pl.BlockSpec((tm, D_MODEL), lambda i: (i, 0))
    return pl.pallas_call(
        _proj1_kernel,
        grid=(n // tm,),
        in_specs=[row, _const_spec(wq.shape), _const_spec(wk.shape), _const_spec(wvt.shape)],
        out_specs=[row, row, pl.BlockSpec((1, D_MODEL, tm), lambda i: (i, 0, 0))],
        out_shape=[jax.ShapeDtypeStruct((n, D_MODEL), BF16), jax.ShapeDtypeStruct((n, D_MODEL), BF16),
                   jax.ShapeDtypeStruct((n // tm, D_MODEL, tm), BF16)],
        compiler_params=pltpu.CompilerParams(dimension_semantics=("arbitrary",), vmem_limit_bytes=VMEM_LIMIT),
        name="proj1",
    )(x2d, wq, wk, wvt)


def _diff_kernel(lam_ref, slope_ref, q_ref, k_ref, vt_ref, feat_ref, g_ref, o_ref,
                 c_ref, s0_ref, s1_ref, p0_ref, p1_ref, *, t, nk):
    h = pl.program_id(1)
    i = pl.program_id(2)
    slope = slope_ref[h]
    lam = lam_ref[0]
    hd = 2 * DIFF_HEAD_DIM
    s_refs = (s0_ref, s1_ref)
    p_refs = (p0_ref, p1_ref)
    ones_rows = 16

    cio = lax.broadcasted_iota(jnp.int32, (1, 2 * t), 1)
    qloc = jnp.where(cio >= t, cio - t, cio)

    @pl.when(i == 0)
    def _():
        kloc = lax.broadcasted_iota(jnp.int32, (t, 2 * t), 0)
        c_ref[...] = (-2.0 * slope) * jnp.maximum(kloc - qloc, 0).astype(F32)

    q = q_ref[...]
    lane = lax.broadcasted_iota(jnp.int32, q.shape, 1)
    lo = lane < DIFF_HEAD_DIM
    z = jnp.zeros_like(q)
    qm = jnp.concatenate([jnp.where(lo, q, z), jnp.where(lo, z, q)], axis=0)
    lane2 = lax.broadcasted_iota(jnp.int32, (2 * t, hd), 1)
    qf = jnp.where(lane2 < 2, slope, 0.0).astype(BF16)
    qa = jnp.concatenate([qm, qf], axis=1)
    ones = jnp.ones((ones_rows, t), BF16)

    def tile_index(jj):
        return i if jj == 0 else (i + jj) & (nk - 1)

    def scores(jj, s_ref):
        j = tile_index(jj)
        above = j > i
        kt = k_ref[pl.ds(pl.multiple_of(j * t, t), t), :]
        ka = jnp.concatenate([kt, feat_ref[jnp.where(above, 1, 0)]], axis=1)
        s = _dot_nt(ka, qa)
        if jj == 0:
            s = s + c_ref[...]
        s_ref[...] = s
        cpos = slope * ((j - i) * t - qloc).astype(F32)
        cvec = jnp.where(above, -cpos, cpos)
        return jnp.max(s, axis=0, keepdims=True) + cvec, cvec

    smax, cvec = scores(0, s_refs[0])
    m = acc = None
    for jj in range(nk):
        nxt = scores(jj + 1, s_refs[(jj + 1) % 2]) if jj + 1 < nk else None
        m_new = smax if m is None else jnp.maximum(m, smax)
        p_ref = p_refs[jj % 2]
        p_ref[...] = jnp.exp(s_refs[jj % 2][...] - (m_new - cvec)).astype(BF16)
        vta = jnp.concatenate([vt_ref[tile_index(jj)], ones], axis=0)
        r = _dot(vta, p_ref[...])
        acc = r if acc is None else jnp.exp(m - m_new) * acc + r
        m = m_new
        if nxt is not None:
            smax, cvec = nxt

    inv1 = 1.0 / acc[hd:hd + 1, :t]
    inv2 = 1.0 / acc[hd:hd + 1, t:]
    o = acc[:hd, :t] * inv1 - lam * (acc[:hd, t:] * inv2)
    ms = jnp.mean(o * o, axis=0, keepdims=True)
    o = o * lax.rsqrt(ms + LN_EPS) * g_ref[...] * (1.0 - LAM_INIT_1)
    o_ref[...] = o.T.astype(BF16)


def _diff_features(t):
    a = jnp.arange(t, dtype=jnp.int32)
    lo = a % 256
    f = jnp.zeros((t, 2 * DIFF_HEAD_DIM), F32).at[:, 0].set(lo.astype(F32)).at[:, 1].set((a - lo).astype(F32))
    return jnp.stack([f, -f]).astype(BF16)


def _diffattn(lam, slopes, q, k, vt, g_col, batch, seq):
    t = TK_DIFF
    nk = seq // t
    assert nk & (nk - 1) == 0
    hd = 2 * DIFF_HEAD_DIM
    feat = _diff_features(t)
    smem = pl.BlockSpec(memory_space=pltpu.SMEM)
    return pl.pallas_call(
        functools.partial(_diff_kernel, t=t, nk=nk),
        grid=(batch, DIFF_HEADS, nk),
        in_specs=[smem, smem,
                  pl.BlockSpec((t, hd), lambda b, h, i: (b * nk + i, h)),
                  pl.BlockSpec((seq, hd), lambda b, h, i: (b, h)),
                  pl.BlockSpec((nk, hd, t), lambda b, h, i: (b, h, 0)),
                  _const_spec(feat.shape),
                  _const_spec(g_col.shape)],
        out_specs=pl.BlockSpec((t, hd), lambda b, h, i: (b * nk + i, h)),
        out_shape=jax.ShapeDtypeStruct(q.shape, BF16),
        scratch_shapes=[pltpu.VMEM((t, 2 * t), F32)] * 3 + [pltpu.VMEM((t, 2 * t), BF16)] * 2,
        compiler_params=pltpu.CompilerParams(dimension_semantics=("arbitrary",) * 3,
                                             vmem_limit_bytes=VMEM_LIMIT),
        name="diffattn",
    )(lam, slopes, q, k, vt, feat, g_col)


def kernel(x_prompt, x_sample, l0_w_in, l0_w_out, l0_gate_ln_g, l0_gate_ln_b, l0_w_spatial, l0_b_spatial, l0_na_rpb, l0_ln1_g, l0_ln1_b, l0_w_ff1, l0_w_ff2, l0_ln2_g, l0_ln2_b, l1_w_in, l1_w_out, l1_lambda_q1, l1_lambda_k1, l1_lambda_q2, l1_lambda_k2, l1_subln_g, l1_ln1_g, l1_ln1_b, l1_w_ff1, l1_w_ff2, l1_ln2_g, l1_ln2_b):
    row = lambda p: p.reshape(1, -1).astype(F32)
    w_in0 = l0_w_in.astype(BF16)
    w_out0a = l0_w_out[:A_WIDTH].astype(BF16)
    w_out0b = l0_w_out[A_WIDTH:].astype(BF16)
    w_s = l0_w_spatial.astype(BF16)
    b_s = jnp.broadcast_to(l0_b_spatial.astype(F32)[:, :, None], (A_GROUPS, CHUNK, LANES))
    na_bias = _na_bias_table(l0_na_rpb.astype(F32))
    wq1 = l1_w_in[:, :D_MODEL].astype(BF16)
    wk1 = l1_w_in[:, D_MODEL:2 * D_MODEL].astype(BF16)
    wvt1 = l1_w_in[:, 2 * D_MODEL:].T.astype(BF16)
    w_out1 = l1_w_out.astype(BF16)
    lam1 = jnp.exp(jnp.sum((l1_lambda_q1 * l1_lambda_k1).astype(F32)))
    lam2 = jnp.exp(jnp.sum((l1_lambda_q2 * l1_lambda_k2).astype(F32)))
    lam = (lam1 - lam2 + LAM_INIT_1).reshape(1).astype(F32)
    slopes = jnp.exp2(-8.0 * jnp.arange(1, DIFF_HEADS + 1, dtype=F32) / DIFF_HEADS)
    g_col = l1_subln_g.astype(F32).reshape(-1, 1)
    ff0 = (l0_w_ff1.astype(BF16), l0_w_ff2.astype(BF16))
    ff1 = (l1_w_ff1.astype(BF16), l1_w_ff2.astype(BF16))

    def trunk(x):
        batch, seq, _ = x.shape
        x2d = x.reshape(batch * seq, D_MODEL)
        out_a, q, k, v = _proj0(x2d, w_in0, row(l0_gate_ln_g), row(l0_gate_ln_b), w_s, b_s)
        out_b = _natten(q, k, v, na_bias, batch, seq)
        x2d = _post([out_a, out_b], [w_out0a, w_out0b], x2d, row(l0_ln1_g), row(l0_ln1_b),
                    ff0[0], ff0[1], row(l0_ln2_g), row(l0_ln2_b))
        q, k, vt = _proj1(x2d, wq1, wk1, wvt1)
        o = _diffattn(lam, slopes, q, k, vt, g_col, batch, seq)
        x2d = _post([o], [w_out1], x2d, row(l1_ln1_g), row(l1_ln1_b),
                    ff1[0], ff1[1], row(l1_ln2_g), row(l1_ln2_b))
        return x2d.reshape(batch, seq, D_MODEL)

    return (trunk(x_prompt), trunk(x_sample))
```

```python
import functools
import math

import jax
import jax.numpy as jnp
import numpy as np
from jax import lax
from jax.experimental import pallas as pl
from jax.experimental.pallas import tpu as pltpu

F32 = jnp.float32
BF16 = jnp.bfloat16

D_MODEL = 1024
DEPTH = 2
GRID_W = 64
CHUNK = 128
A_WIDTH = 512
A_GROUPS = 4
NA_HEADS = 8
NA_HEAD_DIM = 64
NA_WIDTH = 512
NA_KH = 8
NA_KW = 16
DIFF_HEADS = 8
DIFF_HEAD_DIM = 64
D_FF = 4 * D_MODEL
LN_EPS = 1e-5
ALPHA = (2 * DEPTH) ** 0.25
LAM_INIT_1 = 0.8 - 0.6 * math.exp(-0.3 * 1)
NEG = -1e30
LAG_MARGIN = 60.0

LANES = 128
BF16_EXACT_INT = 256
VMEM_LIMIT = 56 * 1024 * 1024

TM_PROJ0 = 512
TM_POST = 1024
SUB_POST = 256
TK_DIFF = 512
NA_ROWS_PER_STEP = 8
FF_CHUNK = 1024


def _const_spec(shape):
    nd = len(shape)
    return pl.BlockSpec(shape, lambda *_: (0,) * nd, pipeline_mode=pl.Buffered(1))


def _layer_norm(y, g, b):
    mu = jnp.mean(y, axis=-1, keepdims=True)
    yc = y - mu
    var = jnp.mean(yc * yc, axis=-1, keepdims=True)
    return yc * lax.rsqrt(var + LN_EPS) * g + b


def _dot(a, b):
    return jnp.dot(a, b, preferred_element_type=F32)


def _dot_nt(a, b):
    return lax.dot_general(a, b, (((1,), (1,)), ((), ())), preferred_element_type=F32)


def _proj0_kernel(x_ref, w_ref, lng_ref, lnb_ref, ws_ref, bs_ref, oa_ref, q_ref, k_ref, v_ref):
    tm = x_ref.shape[0]
    xb = x_ref[...].astype(BF16)

    def mm(lo, hi):
        return _dot(xb, w_ref[:, lo:hi])

    u = jax.nn.gelu(mm(0, A_WIDTH))
    v = jax.nn.gelu(mm(A_WIDTH, 2 * A_WIDTH))
    vn = _layer_norm(v, lng_ref[...], lnb_ref[...]).astype(BF16)
    for c in range(tm // CHUNK):
        rs = slice(c * CHUNK, (c + 1) * CHUNK)
        for g in range(A_GROUPS):
            cs = slice(g * LANES, (g + 1) * LANES)
            mixed = _dot(ws_ref[g], vn[rs, cs]) + bs_ref[g]
            oa_ref[rs, cs] = (u[rs, cs] * mixed).astype(BF16)
    base = 2 * A_WIDTH
    q_ref[...] = (mm(base, base + NA_WIDTH) * (NA_HEAD_DIM ** -0.5)).astype(BF16)
    k_ref[...] = mm(base + NA_WIDTH, base + 2 * NA_WIDTH).astype(BF16)
    v_ref[...] = mm(base + 2 * NA_WIDTH, base + 3 * NA_WIDTH).astype(BF16)


def _proj0(x2d, w_in, ln_g, ln_b, w_s, b_s):
    n = x2d.shape[0]
    tm = TM_PROJ0
    row = lambda width: pl.BlockSpec((tm, width), lambda i: (i, 0))
    out = jax.ShapeDtypeStruct((n, A_WIDTH), BF16)
    return pl.pallas_call(
        _proj0_kernel,
        grid=(n // tm,),
        in_specs=[row(D_MODEL), _const_spec(w_in.shape), _const_spec(ln_g.shape), _const_spec(ln_b.shape),
                  _const_spec(w_s.shape), _const_spec(b_s.shape)],
        out_specs=[row(A_WIDTH)] * 4,
        out_shape=[out] * 4,
        compiler_params=pltpu.CompilerParams(dimension_semantics=("arbitrary",), vmem_limit_bytes=VMEM_LIMIT),
        name="proj0_sgu",
    )(x2d, w_in, ln_g, ln_b, w_s, b_s)


def _natten_kernel(q_ref, k_ref, v_ref, bias_ref, o_ref, *, rows, rb):
    i = pl.program_id(1)
    lane = lax.broadcasted_iota(jnp.int32, (GRID_W, LANES), 1)
    lo = lane < NA_HEAD_DIM
    nkeys = NA_KH * GRID_W
    ones = jnp.ones((nkeys, LANES), BF16)
    for rr in range(rb):
        r = i * rb + rr
        r_start = jnp.clip(r - NA_KH // 2, 0, rows - NA_KH)
        d = r - r_start
        kbase = pl.multiple_of(r_start * GRID_W, GRID_W)
        qrows = slice(rr * GRID_W, (rr + 1) * GRID_W)
        for hp in range(NA_HEADS // 2):
            cs = slice(hp * LANES, (hp + 1) * LANES)
            qp = q_ref[qrows, cs]
            z = jnp.zeros_like(qp)
            qq = jnp.concatenate([jnp.where(lo, qp, z), jnp.where(lo, z, qp)], axis=0)
            kk = k_ref[pl.ds(kbase, nkeys), cs]
            vv = jnp.concatenate([v_ref[pl.ds(kbase, nkeys), cs], ones], axis=1)
            s = _dot_nt(qq, kk) + bias_ref[d, hp]
            m = jnp.max(s, axis=-1, keepdims=True)
            pv = _dot(jnp.exp(s - m).astype(BF16), vv)
            pv = pv[:, :LANES] / pv[:, LANES:]
            o_ref[qrows, cs] = jnp.where(lo, pv[:GRID_W], pv[GRID_W:]).astype(BF16)


def _natten(q, k, v, bias, batch, seq):
    rows = seq // GRID_W
    rb = NA_ROWS_PER_STEP
    nblk = rows // rb
    kern = functools.partial(_natten_kernel, rows=rows, rb=rb)
    qspec = pl.BlockSpec((rb * GRID_W, NA_WIDTH), lambda b, i: (b * nblk + i, 0))
    kvspec = pl.BlockSpec((seq, NA_WIDTH), lambda b, i: (b, 0))
    return pl.pallas_call(
        kern,
        grid=(batch, nblk),
        in_specs=[qspec, kvspec, kvspec, _const_spec(bias.shape)],
        out_specs=qspec,
        out_shape=jax.ShapeDtypeStruct(q.shape, BF16),
        compiler_params=pltpu.CompilerParams(dimension_semantics=("arbitrary", "arbitrary"),
                                             vmem_limit_bytes=VMEM_LIMIT),
        name="natten",
    )(q, k, v, bias)


def _na_bias_table(rpb):
    c = np.arange(GRID_W)
    kc = np.arange(GRID_W)
    c_start = np.clip(c - NA_KW // 2, 0, GRID_W - NA_KW)
    valid = (kc[None, :] >= c_start[:, None]) & (kc[None, :] < c_start[:, None] + NA_KW)
    coff = kc[None, :] - c[:, None] + (NA_KW - 1)
    sel = (valid[None] & (coff[None] == np.arange(2 * NA_KW - 1)[:, None, None])).astype(np.float32)
    t = jnp.einsum("hro,ock->hrck", rpb, jnp.asarray(sel), precision=lax.Precision.HIGHEST)
    t = t + jnp.asarray(np.where(valid, 0.0, NEG).astype(np.float32))[None, None]
    t = jnp.stack([t[:, NA_KH - 1 - d:2 * NA_KH - 1 - d] for d in range(NA_KH)])
    t = jnp.transpose(t, (0, 1, 3, 2, 4))
    return t.reshape(NA_KH, NA_HEADS // 2, 2 * GRID_W, NA_KH * GRID_W).astype(F32)


def _post_kernel(*refs, n_in, sub):
    a_refs = refs[:n_in]
    wo_refs = refs[n_in:2 * n_in]
    x_ref, g1_ref, b1_ref, w1_ref, w2_ref, g2_ref, b2_ref, o_ref = refs[2 * n_in:]
    subs = [slice(k * sub, (k + 1) * sub) for k in range(x_ref.shape[0] // sub)]
    x1s = []
    for rs in subs:
        mix = _dot(a_refs[0][rs, :], wo_refs[0][...])
        for a_ref, wo_ref in zip(a_refs[1:], wo_refs[1:]):
            mix = mix + _dot(a_ref[rs, :], wo_ref[...])
        x1s.append(_layer_norm(ALPHA * x_ref[rs, :] + mix, g1_ref[...], b1_ref[...]))
    for rs, x1 in zip(subs, x1s):
        x1b = x1.astype(BF16)
        acc = None
        for j in range(D_FF // FF_CHUNK):
            fs = slice(j * FF_CHUNK, (j + 1) * FF_CHUNK)
            hid = jnp.square(jnp.maximum(_dot(x1b, w1_ref[:, fs]), 0.0)).astype(BF16)
            part = _dot(hid, w2_ref[fs, :])
            acc = part if acc is None else acc + part
        o_ref[rs, :] = _layer_norm(ALPHA * x1 + acc, g2_ref[...], b2_ref[...])


def _post(acts, w_outs, x2d, g1, b1, w1, w2, g2, b2):
    n = x2d.shape[0]
    tm = TM_POST
    n_in = len(acts)
    row = lambda width: pl.BlockSpec((tm, width), lambda i: (i, 0))
    in_specs = ([row(a.shape[1]) for a in acts] + [_const_spec(w.shape) for w in w_outs]
                + [row(D_MODEL)] + [_const_spec(p.shape) for p in (g1, b1, w1, w2, g2, b2)])
    return pl.pallas_call(
        functools.partial(_post_kernel, n_in=n_in, sub=SUB_POST),
        grid=(n // tm,),
        in_specs=in_specs,
        out_specs=row(D_MODEL),
        out_shape=jax.ShapeDtypeStruct((n, D_MODEL), F32),
        compiler_params=pltpu.CompilerParams(dimension_semantics=("arbitrary",), vmem_limit_bytes=VMEM_LIMIT),
        name="post",
    )(*acts, *w_outs, x2d, g1, b1, w1, w2, g2, b2)


def _proj1_kernel(x_ref, wq_ref, wk_ref, wvt_ref, q_ref, k_ref, vt_ref):
    xb = x_ref[...].astype(BF16)
    q_ref[...] = (_dot(xb, wq_ref[...]) * (DIFF_HEAD_DIM ** -0.5)).astype(BF16)
    k_ref[...] = _dot(xb, wk_ref[...]).astype(BF16)
    vt_ref[0] = _dot_nt(wvt_ref[...], xb).astype(BF16)


def _proj1(x2d, wq, wk, wvt):
    n = x2d.shape[0]
    tm = TK_DIFF
    row = pl.BlockSpec((tm, D_MODEL), lambda i: (i, 0))
    return pl.pallas_call(
        _proj1_kernel,
        grid=(n // tm,),
        in_specs=[row, _const_spec(wq.shape), _const_spec(wk.shape), _const_spec(wvt.shape)],
        out_specs=[row, row, pl.BlockSpec((1, D_MODEL, tm), lambda i: (i, 0, 0))],
        out_shape=[jax.ShapeDtypeStruct((n, D_MODEL), BF16), jax.ShapeDtypeStruct((n, D_MODEL), BF16),
                   jax.ShapeDtypeStruct((n // tm, D_MODEL, tm), BF16)],
        compiler_params=pltpu.CompilerParams(dimension_semantics=("arbitrary",), vmem_limit_bytes=VMEM_LIMIT),
        name="proj1",
    )(x2d, wq, wk, wvt)


def _diff_kernel(lam_ref, slope_ref, q_ref, k_ref, vt_ref, feat_ref, g_ref, o_ref,
                 c_ref, acc_ref, *, t, nk):
    h = pl.program_id(1)
    i = pl.program_id(2)
    slope = slope_ref[h]
    lam = lam_ref[0]
    hd = 2 * DIFF_HEAD_DIM
    ones_rows = 16

    cio = lax.broadcasted_iota(jnp.int32, (1, 2 * t), 1)
    qloc = jnp.where(cio >= t, cio - t, cio)

    @pl.when(i == 0)
    def _():
        kloc = lax.broadcasted_iota(jnp.int32, (t, 2 * t), 0)
        c_ref[...] = (-2.0 * slope) * jnp.maximum(kloc - qloc, 0).astype(F32)

    q = q_ref[...]
    lane = lax.broadcasted_iota(jnp.int32, q.shape, 1)
    lo = lane < DIFF_HEAD_DIM
    z = jnp.zeros_like(q)
    qm = jnp.concatenate([jnp.where(lo, q, z), jnp.where(lo, z, q)], axis=0)
    lane2 = lax.broadcasted_iota(jnp.int32, (2 * t, hd), 1)
    qf = jnp.where(lane2 < 2, slope, 0.0).astype(BF16)
    qa = jnp.concatenate([qm, qf], axis=1)
    ones = jnp.ones((ones_rows, t), BF16)

    def tile_index(jj):
        return i if jj == 0 else (i + jj) & (nk - 1)

    def scores(jj):
        j = tile_index(jj)
        above = j > i
        kt = k_ref[pl.ds(pl.multiple_of(j * t, t), t), :]
        ka = jnp.concatenate([kt, feat_ref[jnp.where(above, 1, 0)]], axis=1)
        s = _dot_nt(ka, qa)
        if jj == 0:
            s = s + c_ref[...]
        cpos = slope * ((j - i) * t - qloc).astype(F32)
        cvec = jnp.where(above, -cpos, cpos)
        return s, cvec

    def weighted_values(jj, s, ref):
        vta = jnp.concatenate([vt_ref[tile_index(jj)], ones], axis=0)
        return _dot(vta, jnp.exp(s - ref).astype(BF16))

    def first_tile():
        s, cvec = scores(0)
        m = jnp.max(s, axis=0, keepdims=True) + cvec
        return m, weighted_values(0, s, m - cvec)

    def lagged_pass():
        m, acc = first_tile()
        excess = jnp.zeros_like(m)
        for jj in range(1, nk):
            s, cvec = scores(jj)
            r = weighted_values(jj, s, m - cvec)
            smax = jnp.max(s, axis=0, keepdims=True) + cvec
            m_new = jnp.maximum(m, smax)
            excess = jnp.maximum(excess, smax - m)
            acc = jnp.exp(m - m_new) * (acc + r)
            m = m_new
        return acc, excess

    def per_tile_max_pass():
        m, acc = first_tile()
        for jj in range(1, nk):
            s, cvec = scores(jj)
            m_new = jnp.maximum(m, jnp.max(s, axis=0, keepdims=True) + cvec)
            acc = jnp.exp(m - m_new) * acc + weighted_values(jj, s, m_new - cvec)
            m = m_new
        return acc

    acc, excess = lagged_pass()
    acc_ref[...] = acc
    finite = jnp.sum(acc * 0.0) == 0.0
    in_margin = jnp.max(excess) <= LAG_MARGIN

    @pl.when(jnp.logical_not(jnp.logical_and(finite, in_margin)))
    def _():
        acc_ref[...] = per_tile_max_pass()

    acc = acc_ref[...]
    inv1 = 1.0 / acc[hd:hd + 1, :t]
    inv2 = 1.0 / acc[hd:hd + 1, t:]
    o = acc[:hd, :t] * inv1 - lam * (acc[:hd, t:] * inv2)
    ms = jnp.mean(o * o, axis=0, keepdims=True)
    o = o * lax.rsqrt(ms + LN_EPS) * g_ref[...] * (1.0 - LAM_INIT_1)
    o_ref[...] = o.T.astype(BF16)


def _diff_features(t):
    a = jnp.arange(t, dtype=jnp.int32)
    lo = a % BF16_EXACT_INT
    f = jnp.zeros((t, 2 * DIFF_HEAD_DIM), F32).at[:, 0].set(lo.astype(F32)).at[:, 1].set((a - lo).astype(F32))
    return jnp.stack([f, -f]).astype(BF16)


def _diffattn(lam, slopes, q, k, vt, g_col, batch, seq):
    t = TK_DIFF
    nk = seq // t
    assert nk & (nk - 1) == 0
    hd = 2 * DIFF_HEAD_DIM
    feat = _diff_features(t)
    smem = pl.BlockSpec(memory_space=pltpu.SMEM)
    return pl.pallas_call(
        functools.partial(_diff_kernel, t=t, nk=nk),
        grid=(batch, DIFF_HEADS, nk),
        in_specs=[smem, smem,
                  pl.BlockSpec((t, hd), lambda b, h, i: (b * nk + i, h)),
                  pl.BlockSpec((seq, hd), lambda b, h, i: (b, h)),
                  pl.BlockSpec((nk, hd, t), lambda b, h, i: (b, h, 0)),
                  _const_spec(feat.shape),
                  _const_spec(g_col.shape)],
        out_specs=pl.BlockSpec((t, hd), lambda b, h, i: (b * nk + i, h)),
        out_shape=jax.ShapeDtypeStruct(q.shape, BF16),
        scratch_shapes=[pltpu.VMEM((t, 2 * t), F32), pltpu.VMEM((hd + 16, 2 * t), F32)],
        compiler_params=pltpu.CompilerParams(dimension_semantics=("arbitrary",) * 3,
                                             vmem_limit_bytes=VMEM_LIMIT),
        name="diffattn",
    )(lam, slopes, q, k, vt, feat, g_col)


def kernel(x_prompt, x_sample, l0_w_in, l0_w_out, l0_gate_ln_g, l0_gate_ln_b, l0_w_spatial, l0_b_spatial, l0_na_rpb, l0_ln1_g, l0_ln1_b, l0_w_ff1, l0_w_ff2, l0_ln2_g, l0_ln2_b, l1_w_in, l1_w_out, l1_lambda_q1, l1_lambda_k1, l1_lambda_q2, l1_lambda_k2, l1_subln_g, l1_ln1_g, l1_ln1_b, l1_w_ff1, l1_w_ff2, l1_ln2_g, l1_ln2_b):
    row = lambda p: p.reshape(1, -1).astype(F32)
    w_in0 = l0_w_in.astype(BF16)
    w_out0a = l0_w_out[:A_WIDTH].astype(BF16)
    w_out0b = l0_w_out[A_WIDTH:].astype(BF16)
    w_s = l0_w_spatial.astype(BF16)
    b_s = jnp.broadcast_to(l0_b_spatial.astype(F32)[:, :, None], (A_GROUPS, CHUNK, LANES))
    na_bias = _na_bias_table(l0_na_rpb.astype(F32))
    wq1 = l1_w_in[:, :D_MODEL].astype(BF16)
    wk1 = l1_w_in[:, D_MODEL:2 * D_MODEL].astype(BF16)
    wvt1 = l1_w_in[:, 2 * D_MODEL:].T.astype(BF16)
    w_out1 = l1_w_out.astype(BF16)
    lam1 = jnp.exp(jnp.sum((l1_lambda_q1 * l1_lambda_k1).astype(F32)))
    lam2 = jnp.exp(jnp.sum((l1_lambda_q2 * l1_lambda_k2).astype(F32)))
    lam = (lam1 - lam2 + LAM_INIT_1).reshape(1).astype(F32)
    slopes = jnp.exp2(-8.0 * jnp.arange(1, DIFF_HEADS + 1, dtype=F32) / DIFF_HEADS)
    g_col = l1_subln_g.astype(F32).reshape(-1, 1)
    ff0 = (l0_w_ff1.astype(BF16), l0_w_ff2.astype(BF16))
    ff1 = (l1_w_ff1.astype(BF16), l1_w_ff2.astype(BF16))

    def trunk(x):
        batch, seq, _ = x.shape
        x2d = x.reshape(batch * seq, D_MODEL)
        out_a, q, k, v = _proj0(x2d, w_in0, row(l0_gate_ln_g), row(l0_gate_ln_b), w_s, b_s)
        out_b = _natten(q, k, v, na_bias, batch, seq)
        x2d = _post([out_a, out_b], [w_out0a, w_out0b], x2d, row(l0_ln1_g), row(l0_ln1_b),
                    ff0[0], ff0[1], row(l0_ln2_g), row(l0_ln2_b))
        q, k, vt = _proj1(x2d, wq1, wk1, wvt1)
        o = _diffattn(lam, slopes, q, k, vt, g_col, batch, seq)
        x2d = _post([o], [w_out1], x2d, row(l1_ln1_g), row(l1_ln1_b),
                    ff1[0], ff1[1], row(l1_ln2_g), row(l1_ln2_b))
        return x2d.reshape(batch, seq, D_MODEL)

    return (trunk(x_prompt), trunk(x_sample))
```

```python
import functools
import math

import jax
import jax.numpy as jnp
import numpy as np
from jax import lax
from jax.experimental import pallas as pl
from jax.experimental.pallas import tpu as pltpu

F32 = jnp.float32
BF16 = jnp.bfloat16

D_MODEL = 1024
DEPTH = 2
GRID_W = 64
CHUNK = 128
A_WIDTH = 512
A_GROUPS = 4
NA_HEADS = 8
NA_HEAD_DIM = 64
NA_WIDTH = 512
NA_KH = 8
NA_KW = 16
DIFF_HEADS = 8
DIFF_HEAD_DIM = 64
D_FF = 4 * D_MODEL
LN_EPS = 1e-5
ALPHA = (2 * DEPTH) ** 0.25
LAM_INIT_1 = 0.8 - 0.6 * math.exp(-0.3 * 1)
NEG = -1e30
LAG_MARGIN = 60.0

LANES = 128
BF16_EXACT_INT = 256
VMEM_LIMIT = 56 * 1024 * 1024

TM_PROJ0 = 512
TM_POST = 1024
SUB_POST = 256
TK_DIFF = 512
QBLOCKS_DIFF = 4
NA_ROWS_PER_STEP = 8
FF_CHUNK = 1024


def _const_spec(shape):
    nd = len(shape)
    return pl.BlockSpec(shape, lambda *_: (0,) * nd, pipeline_mode=pl.Buffered(1))


def _layer_norm(y, g, b):
    mu = jnp.mean(y, axis=-1, keepdims=True)
    yc = y - mu
    var = jnp.mean(yc * yc, axis=-1, keepdims=True)
    return yc * lax.rsqrt(var + LN_EPS) * g + b


def _dot(a, b):
    return jnp.dot(a, b, preferred_element_type=F32)


def _dot_nt(a, b):
    return lax.dot_general(a, b, (((1,), (1,)), ((), ())), preferred_element_type=F32)


def _proj0_kernel(x_ref, w_ref, lng_ref, lnb_ref, ws_ref, bs_ref, oa_ref, q_ref, k_ref, v_ref):
    tm = x_ref.shape[0]
    xb = x_ref[...].astype(BF16)

    def mm(lo, hi):
        return _dot(xb, w_ref[:, lo:hi])

    u = jax.nn.gelu(mm(0, A_WIDTH))
    v = jax.nn.gelu(mm(A_WIDTH, 2 * A_WIDTH))
    vn = _layer_norm(v, lng_ref[...], lnb_ref[...]).astype(BF16)
    for c in range(tm // CHUNK):
        rs = slice(c * CHUNK, (c + 1) * CHUNK)
        for g in range(A_GROUPS):
            cs = slice(g * LANES, (g + 1) * LANES)
            mixed = _dot(ws_ref[g], vn[rs, cs]) + bs_ref[g]
            oa_ref[rs, cs] = (u[rs, cs] * mixed).astype(BF16)
    base = 2 * A_WIDTH
    q_ref[...] = (mm(base, base + NA_WIDTH) * (NA_HEAD_DIM ** -0.5)).astype(BF16)
    k_ref[...] = mm(base + NA_WIDTH, base + 2 * NA_WIDTH).astype(BF16)
    v_ref[...] = mm(base + 2 * NA_WIDTH, base + 3 * NA_WIDTH).astype(BF16)


def _proj0(x2d, w_in, ln_g, ln_b, w_s, b_s):
    n = x2d.shape[0]
    tm = TM_PROJ0
    row = lambda width: pl.BlockSpec((tm, width), lambda i: (i, 0))
    out = jax.ShapeDtypeStruct((n, A_WIDTH), BF16)
    return pl.pallas_call(
        _proj0_kernel,
        grid=(n // tm,),
        in_specs=[row(D_MODEL), _const_spec(w_in.shape), _const_spec(ln_g.shape), _const_spec(ln_b.shape),
                  _const_spec(w_s.shape), _const_spec(b_s.shape)],
        out_specs=[row(A_WIDTH)] * 4,
        out_shape=[out] * 4,
        compiler_params=pltpu.CompilerParams(dimension_semantics=("arbitrary",), vmem_limit_bytes=VMEM_LIMIT),
        name="proj0_sgu",
    )(x2d, w_in, ln_g, ln_b, w_s, b_s)


def _natten_kernel(q_ref, k_ref, v_ref, bias_ref, o_ref, *, rows, rb):
    i = pl.program_id(1)
    lane = lax.broadcasted_iota(jnp.int32, (GRID_W, LANES), 1)
    lo = lane < NA_HEAD_DIM
    nkeys = NA_KH * GRID_W
    ones = jnp.ones((nkeys, LANES), BF16)
    for rr in range(rb):
        r = i * rb + rr
        r_start = jnp.clip(r - NA_KH // 2, 0, rows - NA_KH)
        d = r - r_start
        kbase = pl.multiple_of(r_start * GRID_W, GRID_W)
        qrows = slice(rr * GRID_W, (rr + 1) * GRID_W)
        for hp in range(NA_HEADS // 2):
            cs = slice(hp * LANES, (hp + 1) * LANES)
            qp = q_ref[qrows, cs]
            z = jnp.zeros_like(qp)
            qq = jnp.concatenate([jnp.where(lo, qp, z), jnp.where(lo, z, qp)], axis=0)
            kk = k_ref[pl.ds(kbase, nkeys), cs]
            vv = jnp.concatenate([v_ref[pl.ds(kbase, nkeys), cs], ones], axis=1)
            s = _dot_nt(qq, kk) + bias_ref[d, hp]
            m = jnp.max(s, axis=-1, keepdims=True)
            pv = _dot(jnp.exp(s - m).astype(BF16), vv)
            pv = pv[:, :LANES] / pv[:, LANES:]
            o_ref[qrows, cs] = jnp.where(lo, pv[:GRID_W], pv[GRID_W:]).astype(BF16)


def _natten(q, k, v, bias, batch, seq):
    rows = seq // GRID_W
    rb = NA_ROWS_PER_STEP
    nblk = rows // rb
    kern = functools.partial(_natten_kernel, rows=rows, rb=rb)
    qspec = pl.BlockSpec((rb * GRID_W, NA_WIDTH), lambda b, i: (b * nblk + i, 0))
    kvspec = pl.BlockSpec((seq, NA_WIDTH), lambda b, i: (b, 0))
    return pl.pallas_call(
        kern,
        grid=(batch, nblk),
        in_specs=[qspec, kvspec, kvspec, _const_spec(bias.shape)],
        out_specs=qspec,
        out_shape=jax.ShapeDtypeStruct(q.shape, BF16),
        compiler_params=pltpu.CompilerParams(dimension_semantics=("arbitrary", "arbitrary"),
                                             vmem_limit_bytes=VMEM_LIMIT),
        name="natten",
    )(q, k, v, bias)


def _na_bias_table(rpb):
    c = np.arange(GRID_W)
    kc = np.arange(GRID_W)
    c_start = np.clip(c - NA_KW // 2, 0, GRID_W - NA_KW)
    valid = (kc[None, :] >= c_start[:, None]) & (kc[None, :] < c_start[:, None] + NA_KW)
    coff = kc[None, :] - c[:, None] + (NA_KW - 1)
    sel = (valid[None] & (coff[None] == np.arange(2 * NA_KW - 1)[:, None, None])).astype(np.float32)
    t = jnp.einsum("hro,ock->hrck", rpb, jnp.asarray(sel), precision=lax.Precision.HIGHEST)
    t = t + jnp.asarray(np.where(valid, 0.0, NEG).astype(np.float32))[None, None]
    t = jnp.stack([t[:, NA_KH - 1 - d:2 * NA_KH - 1 - d] for d in range(NA_KH)])
    t = jnp.transpose(t, (0, 1, 3, 2, 4))
    return t.reshape(NA_KH, NA_HEADS // 2, 2 * GRID_W, NA_KH * GRID_W).astype(F32)


def _post_kernel(*refs, n_in, sub):
    a_refs = refs[:n_in]
    wo_refs = refs[n_in:2 * n_in]
    x_ref, g1_ref, b1_ref, w1_ref, w2_ref, g2_ref, b2_ref, o_ref = refs[2 * n_in:]
    subs = [slice(k * sub, (k + 1) * sub) for k in range(x_ref.shape[0] // sub)]
    x1s = []
    for rs in subs:
        mix = _dot(a_refs[0][rs, :], wo_refs[0][...])
        for a_ref, wo_ref in zip(a_refs[1:], wo_refs[1:]):
            mix = mix + _dot(a_ref[rs, :], wo_ref[...])
        x1s.append(_layer_norm(ALPHA * x_ref[rs, :] + mix, g1_ref[...], b1_ref[...]))
    for rs, x1 in zip(subs, x1s):
        x1b = x1.astype(BF16)
        acc = None
        for j in range(D_FF // FF_CHUNK):
            fs = slice(j * FF_CHUNK, (j + 1) * FF_CHUNK)
            hid = jnp.square(jnp.maximum(_dot(x1b, w1_ref[:, fs]), 0.0)).astype(BF16)
            part = _dot(hid, w2_ref[fs, :])
            acc = part if acc is None else acc + part
        o_ref[rs, :] = _layer_norm(ALPHA * x1 + acc, g2_ref[...], b2_ref[...])


def _post(acts, w_outs, x2d, g1, b1, w1, w2, g2, b2):
    n = x2d.shape[0]
    tm = TM_POST
    n_in = len(acts)
    row = lambda width: pl.BlockSpec((tm, width), lambda i: (i, 0))
    in_specs = ([row(a.shape[1]) for a in acts] + [_const_spec(w.shape) for w in w_outs]
                + [row(D_MODEL)] + [_const_spec(p.shape) for p in (g1, b1, w1, w2, g2, b2)])
    return pl.pallas_call(
        functools.partial(_post_kernel, n_in=n_in, sub=SUB_POST),
        grid=(n // tm,),
        in_specs=in_specs,
        out_specs=row(D_MODEL),
        out_shape=jax.ShapeDtypeStruct((n, D_MODEL), F32),
        compiler_params=pltpu.CompilerParams(dimension_semantics=("arbitrary",), vmem_limit_bytes=VMEM_LIMIT),
        name="post",
    )(*acts, *w_outs, x2d, g1, b1, w1, w2, g2, b2)


def _proj1_kernel(x_ref, wq_ref, wk_ref, wvt_ref, q_ref, k_ref, vt_ref):
    xb = x_ref[...].astype(BF16)
    q_ref[...] = (_dot(xb, wq_ref[...]) * (DIFF_HEAD_DIM ** -0.5)).astype(BF16)
    k_ref[...] = _dot(xb, wk_ref[...]).astype(BF16)
    vt_ref[0] = _dot_nt(wvt_ref[...], xb).astype(BF16)


def _proj1(x2d, wq, wk, wvt):
    n = x2d.shape[0]
    tm = TK_DIFF
    row = pl.BlockSpec((tm, D_MODEL), lambda i: (i, 0))
    return pl.pallas_call(
        _proj1_kernel,
        grid=(n // tm,),
        in_specs=[row, _const_spec(wq.shape), _const_spec(wk.shape), _const_spec(wvt.shape)],
        out_specs=[row, row, pl.BlockSpec((1, D_MODEL, tm), lambda i: (i, 0, 0))],
        out_shape=[jax.ShapeDtypeStruct((n, D_MODEL), BF16), jax.ShapeDtypeStruct((n, D_MODEL), BF16),
                   jax.ShapeDtypeStruct((n // tm, D_MODEL, tm), BF16)],
        compiler_params=pltpu.CompilerParams(dimension_semantics=("arbitrary",), vmem_limit_bytes=VMEM_LIMIT),
        name="proj1",
    )(x2d, wq, wk, wvt)


def _diff_kernel(lam_ref, slope_ref, q_ref, k_ref, vt_ref, feat_ref, g_ref, o_ref, c_ref, *, t, nk, nqb):
    h = pl.program_id(1)
    step = pl.program_id(2)
    slope = slope_ref[h]
    lam = lam_ref[0]
    hd = 2 * DIFF_HEAD_DIM
    ones_rows = 16

    cio = lax.broadcasted_iota(jnp.int32, (1, 2 * t), 1)
    qloc = jnp.where(cio >= t, cio - t, cio)

    @pl.when(step == 0)
    def _():
        kloc = lax.broadcasted_iota(jnp.int32, (t, 2 * t), 0)
        c_ref[...] = (-2.0 * slope) * jnp.maximum(kloc - qloc, 0).astype(F32)

    lane = lax.broadcasted_iota(jnp.int32, (t, hd), 1)
    lo = lane < DIFF_HEAD_DIM
    lane2 = lax.broadcasted_iota(jnp.int32, (2 * t, hd), 1)
    qf = jnp.where(lane2 < 2, slope, 0.0).astype(BF16)
    ones = jnp.ones((ones_rows, t), BF16)

    def passes(qb):
        i = step * nqb + qb
        q = q_ref[qb * t:(qb + 1) * t, :]
        z = jnp.zeros_like(q)
        qm = jnp.concatenate([jnp.where(lo, q, z), jnp.where(lo, z, q)], axis=0)
        qa = jnp.concatenate([qm, qf], axis=1)

        def tile_index(jj):
            return i if jj == 0 else (i + jj) & (nk - 1)

        def scores(jj):
            j = tile_index(jj)
            above = j > i
            kt = k_ref[pl.ds(pl.multiple_of(j * t, t), t), :]
            ka = jnp.concatenate([kt, feat_ref[jnp.where(above, 1, 0)]], axis=1)
            s = _dot_nt(ka, qa)
            if jj == 0:
                s = s + c_ref[...]
            cpos = slope * ((j - i) * t - qloc).astype(F32)
            cvec = jnp.where(above, -cpos, cpos)
            return s, cvec

        def weighted_values(jj, s, ref):
            vta = jnp.concatenate([vt_ref[tile_index(jj)], ones], axis=0)
            return _dot(vta, jnp.exp(s - ref).astype(BF16))

        def first_tile():
            s, cvec = scores(0)
            m = jnp.max(s, axis=0, keepdims=True) + cvec
            return m, weighted_values(0, s, m - cvec)

        def lagged_pass():
            m, acc = first_tile()
            excess = jnp.zeros_like(m)
            for jj in range(1, nk):
                s, cvec = scores(jj)
                r = weighted_values(jj, s, m - cvec)
                smax = jnp.max(s, axis=0, keepdims=True) + cvec
                m_new = jnp.maximum(m, smax)
                excess = jnp.maximum(excess, smax - m)
                acc = jnp.exp(m - m_new) * (acc + r)
                m = m_new
            return acc, excess

        def per_tile_max_pass():
            m, acc = first_tile()
            for jj in range(1, nk):
                s, cvec = scores(jj)
                m_new = jnp.maximum(m, jnp.max(s, axis=0, keepdims=True) + cvec)
                acc = jnp.exp(m - m_new) * acc + weighted_values(jj, s, m_new - cvec)
                m = m_new
            return acc

        return lagged_pass, per_tile_max_pass

    def finish(qb, acc):
        inv1 = 1.0 / acc[hd:hd + 1, :t]
        inv2 = 1.0 / acc[hd:hd + 1, t:]
        o = acc[:hd, :t] * inv1 - lam * (acc[:hd, t:] * inv2)
        ms = jnp.mean(o * o, axis=0, keepdims=True)
        o = o * lax.rsqrt(ms + LN_EPS) * g_ref[...] * (1.0 - LAM_INIT_1)
        o_ref[qb * t:(qb + 1) * t, :] = o.T.astype(BF16)

    blocks = [passes(qb) for qb in range(nqb)]
    ok = None
    for qb, (lagged_pass, _) in enumerate(blocks):
        acc, excess = lagged_pass()
        finish(qb, acc)
        good = jnp.logical_and(jnp.sum(acc * 0.0) == 0.0, jnp.max(excess) <= LAG_MARGIN)
        ok = good if ok is None else jnp.logical_and(ok, good)

    @pl.when(jnp.logical_not(ok))
    def _():
        for qb, (_, per_tile_max_pass) in enumerate(blocks):
            finish(qb, per_tile_max_pass())


def _diff_features(t):
    a = jnp.arange(t, dtype=jnp.int32)
    lo = a % BF16_EXACT_INT
    f = jnp.zeros((t, 2 * DIFF_HEAD_DIM), F32).at[:, 0].set(lo.astype(F32)).at[:, 1].set((a - lo).astype(F32))
    return jnp.stack([f, -f]).astype(BF16)


def _diffattn(lam, slopes, q, k, vt, g_col, batch, seq):
    t = TK_DIFF
    nk = seq // t
    assert nk & (nk - 1) == 0
    hd = 2 * DIFF_HEAD_DIM
    feat = _diff_features(t)
    smem = pl.BlockSpec(memory_space=pltpu.SMEM)
    nqb = QBLOCKS_DIFF
    steps = nk // nqb
    qspec = pl.BlockSpec((nqb * t, hd), lambda b, h, i: (b * steps + i, h))
    return pl.pallas_call(
        functools.partial(_diff_kernel, t=t, nk=nk, nqb=nqb),
        grid=(batch, DIFF_HEADS, steps),
        in_specs=[smem, smem, qspec,
                  pl.BlockSpec((seq, hd), lambda b, h, i: (b, h)),
                  pl.BlockSpec((nk, hd, t), lambda b, h, i: (b, h, 0)),
                  _const_spec(feat.shape),
                  _const_spec(g_col.shape)],
        out_specs=qspec,
        out_shape=jax.ShapeDtypeStruct(q.shape, BF16),
        scratch_shapes=[pltpu.VMEM((t, 2 * t), F32)],
        compiler_params=pltpu.CompilerParams(dimension_semantics=("arbitrary",) * 3,
                                             vmem_limit_bytes=VMEM_LIMIT),
        name="diffattn",
    )(lam, slopes, q, k, vt, feat, g_col)


def kernel(x_prompt, x_sample, l0_w_in, l0_w_out, l0_gate_ln_g, l0_gate_ln_b, l0_w_spatial, l0_b_spatial, l0_na_rpb, l0_ln1_g, l0_ln1_b, l0_w_ff1, l0_w_ff2, l0_ln2_g, l0_ln2_b, l1_w_in, l1_w_out, l1_lambda_q1, l1_lambda_k1, l1_lambda_q2, l1_lambda_k2, l1_subln_g, l1_ln1_g, l1_ln1_b, l1_w_ff1, l1_w_ff2, l1_ln2_g, l1_ln2_b):
    row = lambda p: p.reshape(1, -1).astype(F32)
    w_in0 = l0_w_in.astype(BF16)
    w_out0a = l0_w_out[:A_WIDTH].astype(BF16)
    w_out0b = l0_w_out[A_WIDTH:].astype(BF16)
    w_s = l0_w_spatial.astype(BF16)
    b_s = jnp.broadcast_to(l0_b_spatial.astype(F32)[:, :, None], (A_GROUPS, CHUNK, LANES))
    na_bias = _na_bias_table(l0_na_rpb.astype(F32))
    wq1 = l1_w_in[:, :D_MODEL].astype(BF16)
    wk1 = l1_w_in[:, D_MODEL:2 * D_MODEL].astype(BF16)
    wvt1 = l1_w_in[:, 2 * D_MODEL:].T.astype(BF16)
    w_out1 = l1_w_out.astype(BF16)
    lam1 = jnp.exp(jnp.sum((l1_lambda_q1 * l1_lambda_k1).astype(F32)))
    lam2 = jnp.exp(jnp.sum((l1_lambda_q2 * l1_lambda_k2).astype(F32)))
    lam = (lam1 - lam2 + LAM_INIT_1).reshape(1).astype(F32)
    slopes = jnp.exp2(-8.0 * jnp.arange(1, DIFF_HEADS + 1, dtype=F32) / DIFF_HEADS)
    g_col = l1_subln_g.astype(F32).reshape(-1, 1)
    ff0 = (l0_w_ff1.astype(BF16), l0_w_ff2.astype(BF16))
    ff1 = (l1_w_ff1.astype(BF16), l1_w_ff2.astype(BF16))

    def trunk(x):
        batch, seq, _ = x.shape
        x2d = x.reshape(batch * seq, D_MODEL)
        out_a, q, k, v = _proj0(x2d, w_in0, row(l0_gate_ln_g), row(l0_gate_ln_b), w_s, b_s)
        out_b = _natten(q, k, v, na_bias, batch, seq)
        x2d = _post([out_a, out_b], [w_out0a, w_out0b], x2d, row(l0_ln1_g), row(l0_ln1_b),
                    ff0[0], ff0[1], row(l0_ln2_g), row(l0_ln2_b))
        q, k, vt = _proj1(x2d, wq1, wk1, wvt1)
        o = _diffattn(lam, slopes, q, k, vt, g_col, batch, seq)
        x2d = _post([o], [w_out1], x2d, row(l1_ln1_g), row(l1_ln1_b),
                    ff1[0], ff1[1], row(l1_ln2_g), row(l1_ln2_b))
        return x2d.reshape(batch, seq, D_MODEL)

    return (trunk(x_prompt), trunk(x_sample))
```

```python
import functools
import math

import jax
import jax.numpy as jnp
import numpy as np
from jax import lax
from jax.experimental import pallas as pl
from jax.experimental.pallas import tpu as pltpu

F32 = jnp.float32
BF16 = jnp.bfloat16

D_MODEL = 1024
DEPTH = 2
GRID_W = 64
CHUNK = 128
A_WIDTH = 512
A_GROUPS = 4
NA_HEADS = 8
NA_HEAD_DIM = 64
NA_WIDTH = 512
NA_KH = 8
NA_KW = 16
DIFF_HEADS = 8
DIFF_HEAD_DIM = 64
D_FF = 4 * D_MODEL
LN_EPS = 1e-5
ALPHA = (2 * DEPTH) ** 0.25
LAM_INIT_1 = 0.8 - 0.6 * math.exp(-0.3 * 1)
NEG = -1e30
LAG_MARGIN = 60.0

LANES = 128
BF16_EXACT_INT = 256
VMEM_LIMIT = 56 * 1024 * 1024

TM_PROJ0 = 512
TM_POST = 1024
SUB_POST = 256
TK_DIFF = 512
QBLOCKS_DIFF = 4
NA_ROWS_PER_STEP = 8
FF_CHUNK = 1024


def _const_spec(shape):
    nd = len(shape)
    return pl.BlockSpec(shape, lambda *_: (0,) * nd, pipeline_mode=pl.Buffered(1))


def _layer_norm(y, g, b):
    mu = jnp.mean(y, axis=-1, keepdims=True)
    yc = y - mu
    var = jnp.mean(yc * yc, axis=-1, keepdims=True)
    return yc * lax.rsqrt(var + LN_EPS) * g + b


def _dot(a, b):
    return jnp.dot(a, b, preferred_element_type=F32)


def _dot_nt(a, b):
    return lax.dot_general(a, b, (((1,), (1,)), ((), ())), preferred_element_type=F32)


def _proj0_kernel(x_ref, w_ref, lng_ref, lnb_ref, ws_ref, bs_ref, oa_ref, q_ref, k_ref, v_ref):
    tm = x_ref.shape[0]
    xb = x_ref[...].astype(BF16)

    def mm(lo, hi):
        return _dot(xb, w_ref[:, lo:hi])

    u = jax.nn.gelu(mm(0, A_WIDTH))
    v = jax.nn.gelu(mm(A_WIDTH, 2 * A_WIDTH))
    vn = _layer_norm(v, lng_ref[...], lnb_ref[...]).astype(BF16)
    for c in range(tm // CHUNK):
        rs = slice(c * CHUNK, (c + 1) * CHUNK)
        for g in range(A_GROUPS):
            cs = slice(g * LANES, (g + 1) * LANES)
            mixed = _dot(ws_ref[g], vn[rs, cs]) + bs_ref[g]
            oa_ref[rs, cs] = (u[rs, cs] * mixed).astype(BF16)
    base = 2 * A_WIDTH
    q_ref[...] = (mm(base, base + NA_WIDTH) * (NA_HEAD_DIM ** -0.5)).astype(BF16)
    k_ref[...] = mm(base + NA_WIDTH, base + 2 * NA_WIDTH).astype(BF16)
    v_ref[...] = mm(base + 2 * NA_WIDTH, base + 3 * NA_WIDTH).astype(BF16)


def _proj0(x2d, w_in, ln_g, ln_b, w_s, b_s):
    n = x2d.shape[0]
    tm = TM_PROJ0
    row = lambda width: pl.BlockSpec((tm, width), lambda i: (i, 0))
    out = jax.ShapeDtypeStruct((n, A_WIDTH), BF16)
    return pl.pallas_call(
        _proj0_kernel,
        grid=(n // tm,),
        in_specs=[row(D_MODEL), _const_spec(w_in.shape), _const_spec(ln_g.shape), _const_spec(ln_b.shape),
                  _const_spec(w_s.shape), _const_spec(b_s.shape)],
        out_specs=[row(A_WIDTH)] * 4,
        out_shape=[out] * 4,
        compiler_params=pltpu.CompilerParams(dimension_semantics=("arbitrary",), vmem_limit_bytes=VMEM_LIMIT),
        name="proj0_sgu",
    )(x2d, w_in, ln_g, ln_b, w_s, b_s)


def _natten_kernel(q_ref, k_ref, v_ref, bias_ref, o_ref, *, rows, rb):
    i = pl.program_id(1)
    lane = lax.broadcasted_iota(jnp.int32, (GRID_W, LANES), 1)
    lo = lane < NA_HEAD_DIM
    nkeys = NA_KH * GRID_W
    ones = jnp.ones((nkeys, LANES), BF16)
    for rr in range(rb):
        r = i * rb + rr
        r_start = jnp.clip(r - NA_KH // 2, 0, rows - NA_KH)
        d = r - r_start
        kbase = pl.multiple_of(r_start * GRID_W, GRID_W)
        qrows = slice(rr * GRID_W, (rr + 1) * GRID_W)
        for hp in range(NA_HEADS // 2):
            cs = slice(hp * LANES, (hp + 1) * LANES)
            qp = q_ref[qrows, cs]
            z = jnp.zeros_like(qp)
            qq = jnp.concatenate([jnp.where(lo, qp, z), jnp.where(lo, z, qp)], axis=0)
            kk = k_ref[pl.ds(kbase, nkeys), cs]
            vv = jnp.concatenate([v_ref[pl.ds(kbase, nkeys), cs], ones], axis=1)
            s = _dot_nt(qq, kk) + bias_ref[d, hp]
            m = jnp.max(s, axis=-1, keepdims=True)
            pv = _dot(jnp.exp(s - m).astype(BF16), vv)
            pv = pv[:, :LANES] / pv[:, LANES:]
            o_ref[qrows, cs] = jnp.where(lo, pv[:GRID_W], pv[GRID_W:]).astype(BF16)


def _natten(q, k, v, bias, batch, seq):
    rows = seq // GRID_W
    rb = NA_ROWS_PER_STEP
    nblk = rows // rb
    kern = functools.partial(_natten_kernel, rows=rows, rb=rb)
    qspec = pl.BlockSpec((rb * GRID_W, NA_WIDTH), lambda b, i: (b * nblk + i, 0))
    kvspec = pl.BlockSpec((seq, NA_WIDTH), lambda b, i: (b, 0))
    return pl.pallas_call(
        kern,
        grid=(batch, nblk),
        in_specs=[qspec, kvspec, kvspec, _const_spec(bias.shape)],
        out_specs=qspec,
        out_shape=jax.ShapeDtypeStruct(q.shape, BF16),
        compiler_params=pltpu.CompilerParams(dimension_semantics=("arbitrary", "arbitrary"),
                                             vmem_limit_bytes=VMEM_LIMIT),
        name="natten",
    )(q, k, v, bias)


def _na_bias_table(rpb):
    c = np.arange(GRID_W)
    kc = np.arange(GRID_W)
    c_start = np.clip(c - NA_KW // 2, 0, GRID_W - NA_KW)
    valid = (kc[None, :] >= c_start[:, None]) & (kc[None, :] < c_start[:, None] + NA_KW)
    coff = kc[None, :] - c[:, None] + (NA_KW - 1)
    sel = (valid[None] & (coff[None] == np.arange(2 * NA_KW - 1)[:, None, None])).astype(np.float32)
    t = jnp.einsum("hro,ock->hrck", rpb, jnp.asarray(sel), precision=lax.Precision.HIGHEST)
    t = t + jnp.asarray(np.where(valid, 0.0, NEG).astype(np.float32))[None, None]
    t = jnp.stack([t[:, NA_KH - 1 - d:2 * NA_KH - 1 - d] for d in range(NA_KH)])
    t = jnp.transpose(t, (0, 1, 3, 2, 4))
    return t.reshape(NA_KH, NA_HEADS // 2, 2 * GRID_W, NA_KH * GRID_W).astype(F32)


def _post_kernel(*refs, n_in, sub):
    a_refs = refs[:n_in]
    wo_ref, x_ref, g1_ref, b1_ref, w1_ref, w2_ref, g2_ref, b2_ref, o_ref = refs[n_in:]
    subs = [slice(k * sub, (k + 1) * sub) for k in range(x_ref.shape[0] // sub)]
    x1s = []
    for rs in subs:
        a = jnp.concatenate([a_ref[rs, :] for a_ref in a_refs], axis=1)
        mix = _dot(a, wo_ref[...])
        x1s.append(_layer_norm(ALPHA * x_ref[rs, :] + mix, g1_ref[...], b1_ref[...]))
    for rs, x1 in zip(subs, x1s):
        x1b = x1.astype(BF16)
        acc = None
        for j in range(D_FF // FF_CHUNK):
            fs = slice(j * FF_CHUNK, (j + 1) * FF_CHUNK)
            hid = jnp.square(jnp.maximum(_dot(x1b, w1_ref[:, fs]), 0.0)).astype(BF16)
            part = _dot(hid, w2_ref[fs, :])
            acc = part if acc is None else acc + part
        o_ref[rs, :] = _layer_norm(ALPHA * x1 + acc, g2_ref[...], b2_ref[...])


def _post(acts, w_out, x2d, g1, b1, w1, w2, g2, b2):
    n = x2d.shape[0]
    tm = TM_POST
    n_in = len(acts)
    row = lambda width: pl.BlockSpec((tm, width), lambda i: (i, 0))
    in_specs = ([row(a.shape[1]) for a in acts] + [_const_spec(w_out.shape), row(D_MODEL)]
                + [_const_spec(p.shape) for p in (g1, b1, w1, w2, g2, b2)])
    return pl.pallas_call(
        functools.partial(_post_kernel, n_in=n_in, sub=SUB_POST),
        grid=(n // tm,),
        in_specs=in_specs,
        out_specs=row(D_MODEL),
        out_shape=jax.ShapeDtypeStruct((n, D_MODEL), F32),
        compiler_params=pltpu.CompilerParams(dimension_semantics=("arbitrary",), vmem_limit_bytes=VMEM_LIMIT),
        name="post",
    )(*acts, w_out, x2d, g1, b1, w1, w2, g2, b2)


def _proj1_kernel(x_ref, wq_ref, wk_ref, wvt_ref, q_ref, k_ref, vt_ref):
    xb = x_ref[...].astype(BF16)
    q_ref[...] = (_dot(xb, wq_ref[...]) * (DIFF_HEAD_DIM ** -0.5)).astype(BF16)
    k_ref[...] = _dot(xb, wk_ref[...]).astype(BF16)
    vt_ref[0] = _dot_nt(wvt_ref[...], xb).astype(BF16)


def _proj1(x2d, wq, wk, wvt):
    n = x2d.shape[0]
    tm = TK_DIFF
    row = pl.BlockSpec((tm, D_MODEL), lambda i: (i, 0))
    return pl.pallas_call(
        _proj1_kernel,
        grid=(n // tm,),
        in_specs=[row, _const_spec(wq.shape), _const_spec(wk.shape), _const_spec(wvt.shape)],
        out_specs=[row, row, pl.BlockSpec((1, D_MODEL, tm), lambda i: (i, 0, 0))],
        out_shape=[jax.ShapeDtypeStruct((n, D_MODEL), BF16), jax.ShapeDtypeStruct((n, D_MODEL), BF16),
                   jax.ShapeDtypeStruct((n // tm, D_MODEL, tm), BF16)],
        compiler_params=pltpu.CompilerParams(dimension_semantics=("arbitrary",), vmem_limit_bytes=VMEM_LIMIT),
        name="proj1",
    )(x2d, wq, wk, wvt)


def _diff_kernel(lam_ref, slope_ref, q_ref, k_ref, vt_ref, feat_ref, g_ref, o_ref, c_ref, *, t, nk, nqb):
    h = pl.program_id(1)
    step = pl.program_id(2)
    slope = slope_ref[h]
    lam = lam_ref[0]
    hd = 2 * DIFF_HEAD_DIM
    ones_rows = 16

    cio = lax.broadcasted_iota(jnp.int32, (1, 2 * t), 1)
    qloc = jnp.where(cio >= t, cio - t, cio)

    @pl.when(step == 0)
    def _():
        kloc = lax.broadcasted_iota(jnp.int32, (t, 2 * t), 0)
        c_ref[...] = (-2.0 * slope) * jnp.maximum(kloc - qloc, 0).astype(F32)

    lane = lax.broadcasted_iota(jnp.int32, (t, hd), 1)
    lo = lane < DIFF_HEAD_DIM
    lane2 = lax.broadcasted_iota(jnp.int32, (2 * t, hd), 1)
    qf = jnp.where(lane2 < 2, slope, 0.0).astype(BF16)
    ones = jnp.ones((ones_rows, t), BF16)

    def block(qb):
        i = step * nqb + qb
        q = q_ref[pl.ds(pl.multiple_of(qb * t, t), t), :]
        z = jnp.zeros_like(q)
        qm = jnp.concatenate([jnp.where(lo, q, z), jnp.where(lo, z, q)], axis=0)
        qa = jnp.concatenate([qm, qf], axis=1)

        def tile_index(jj):
            return jnp.where(jj == 0, i, (i + jj) & (nk - 1))

        def scores(jj):
            j = tile_index(jj)
            above = j > i
            kt = k_ref[pl.ds(pl.multiple_of(j * t, t), t), :]
            ka = jnp.concatenate([kt, feat_ref[jnp.where(above, 1, 0)]], axis=1)
            s = _dot_nt(ka, qa)
            if isinstance(jj, int):
                s = s + c_ref[...] if jj == 0 else s
            else:
                s = s + jnp.where(jj == 0, 1.0, 0.0) * c_ref[...]
            cpos = slope * ((j - i) * t - qloc).astype(F32)
            cvec = jnp.where(above, -cpos, cpos)
            return s, cvec

        def weighted_values(jj, s, ref):
            vta = jnp.concatenate([vt_ref[tile_index(jj)], ones], axis=0)
            return _dot(vta, jnp.exp(s - ref).astype(BF16))

        def lagged_pass():
            s, cvec = scores(0)
            m = jnp.max(s, axis=0, keepdims=True) + cvec
            acc = weighted_values(0, s, m - cvec)
            excess = jnp.zeros_like(m)
            for jj in range(1, nk):
                s, cvec = scores(jj)
                r = weighted_values(jj, s, m - cvec)
                smax = jnp.max(s, axis=0, keepdims=True) + cvec
                m_new = jnp.maximum(m, smax)
                excess = jnp.maximum(excess, smax - m)
                acc = jnp.exp(m - m_new) * (acc + r)
                m = m_new
            return acc, excess

        def per_tile_max_pass():
            def tile(jj, carry):
                m, acc = carry
                s, cvec = scores(jj)
                m_new = jnp.maximum(m, jnp.max(s, axis=0, keepdims=True) + cvec)
                return m_new, jnp.exp(m - m_new) * acc + weighted_values(jj, s, m_new - cvec)

            init = (jnp.full((1, 2 * t), -jnp.inf, F32), jnp.zeros((hd + ones_rows, 2 * t), F32))
            return lax.fori_loop(0, nk, tile, init)[1]

        return lagged_pass, per_tile_max_pass

    def finish(qb, acc):
        inv1 = 1.0 / acc[hd:hd + 1, :t]
        inv2 = 1.0 / acc[hd:hd + 1, t:]
        o = acc[:hd, :t] * inv1 - lam * (acc[:hd, t:] * inv2)
        ms = jnp.mean(o * o, axis=0, keepdims=True)
        o = o * lax.rsqrt(ms + LN_EPS) * g_ref[...] * (1.0 - LAM_INIT_1)
        o_ref[pl.ds(pl.multiple_of(qb * t, t), t), :] = o.T.astype(BF16)

    ok = None
    for qb in range(nqb):
        acc, excess = block(qb)[0]()
        finish(qb, acc)
        good = jnp.logical_and(jnp.sum(acc * 0.0) == 0.0, jnp.max(excess) <= LAG_MARGIN)
        ok = good if ok is None else jnp.logical_and(ok, good)

    @pl.when(jnp.logical_not(ok))
    def _():
        @pl.loop(0, nqb)
        def _(qb):
            finish(qb, block(qb)[1]())


def _diff_features(t):
    a = jnp.arange(t, dtype=jnp.int32)
    lo = a % BF16_EXACT_INT
    f = jnp.zeros((t, 2 * DIFF_HEAD_DIM), F32).at[:, 0].set(lo.astype(F32)).at[:, 1].set((a - lo).astype(F32))
    return jnp.stack([f, -f]).astype(BF16)


def _diffattn(lam, slopes, q, k, vt, g_col, batch, seq):
    t = TK_DIFF
    nk = seq // t
    assert nk & (nk - 1) == 0
    hd = 2 * DIFF_HEAD_DIM
    feat = _diff_features(t)
    smem = pl.BlockSpec(memory_space=pltpu.SMEM)
    nqb = QBLOCKS_DIFF
    steps = nk // nqb
    qspec = pl.BlockSpec((nqb * t, hd), lambda b, h, i: (b * steps + i, h))
    return pl.pallas_call(
        functools.partial(_diff_kernel, t=t, nk=nk, nqb=nqb),
        grid=(batch, DIFF_HEADS, steps),
        in_specs=[smem, smem, qspec,
                  pl.BlockSpec((seq, hd), lambda b, h, i: (b, h)),
                  pl.BlockSpec((nk, hd, t), lambda b, h, i: (b, h, 0)),
                  _const_spec(feat.shape),
                  _const_spec(g_col.shape)],
        out_specs=qspec,
        out_shape=jax.ShapeDtypeStruct(q.shape, BF16),
        scratch_shapes=[pltpu.VMEM((t, 2 * t), F32)],
        compiler_params=pltpu.CompilerParams(dimension_semantics=("arbitrary",) * 3,
                                             vmem_limit_bytes=VMEM_LIMIT),
        name="diffattn",
    )(lam, slopes, q, k, vt, feat, g_col)


def kernel(x_prompt, x_sample, l0_w_in, l0_w_out, l0_gate_ln_g, l0_gate_ln_b, l0_w_spatial, l0_b_spatial, l0_na_rpb, l0_ln1_g, l0_ln1_b, l0_w_ff1, l0_w_ff2, l0_ln2_g, l0_ln2_b, l1_w_in, l1_w_out, l1_lambda_q1, l1_lambda_k1, l1_lambda_q2, l1_lambda_k2, l1_subln_g, l1_ln1_g, l1_ln1_b, l1_w_ff1, l1_w_ff2, l1_ln2_g, l1_ln2_b):
    row = lambda p: p.reshape(1, -1).astype(F32)
    w_in0 = l0_w_in.astype(BF16)
    w_out0 = l0_w_out.astype(BF16)
    w_s = l0_w_spatial.astype(BF16)
    b_s = jnp.broadcast_to(l0_b_spatial.astype(F32)[:, :, None], (A_GROUPS, CHUNK, LANES))
    na_bias = _na_bias_table(l0_na_rpb.astype(F32))
    wq1 = l1_w_in[:, :D_MODEL].astype(BF16)
    wk1 = l1_w_in[:, D_MODEL:2 * D_MODEL].astype(BF16)
    wvt1 = l1_w_in[:, 2 * D_MODEL:].T.astype(BF16)
    w_out1 = l1_w_out.astype(BF16)
    lam1 = jnp.exp(jnp.sum((l1_lambda_q1 * l1_lambda_k1).astype(F32)))
    lam2 = jnp.exp(jnp.sum((l1_lambda_q2 * l1_lambda_k2).astype(F32)))
    lam = (lam1 - lam2 + LAM_INIT_1).reshape(1).astype(F32)
    slopes = jnp.exp2(-8.0 * jnp.arange(1, DIFF_HEADS + 1, dtype=F32) / DIFF_HEADS)
    g_col = l1_subln_g.astype(F32).reshape(-1, 1)
    ff0 = (l0_w_ff1.astype(BF16), l0_w_ff2.astype(BF16))
    ff1 = (l1_w_ff1.astype(BF16), l1_w_ff2.astype(BF16))

    def trunk(x):
        batch, seq, _ = x.shape
        x2d = x.reshape(batch * seq, D_MODEL)
        out_a, q, k, v = _proj0(x2d, w_in0, row(l0_gate_ln_g), row(l0_gate_ln_b), w_s, b_s)
        out_b = _natten(q, k, v, na_bias, batch, seq)
        x2d = _post([out_a, out_b], w_out0, x2d, row(l0_ln1_g), row(l0_ln1_b),
                    ff0[0], ff0[1], row(l0_ln2_g), row(l0_ln2_b))
        q, k, vt = _proj1(x2d, wq1, wk1, wvt1)
        o = _diffattn(lam, slopes, q, k, vt, g_col, batch, seq)
        x2d = _post([o], w_out1, x2d, row(l1_ln1_g), row(l1_ln1_b),
                    ff1[0], ff1[1], row(l1_ln2_g), row(l1_ln2_b))
        return x2d.reshape(batch, seq, D_MODEL)

    return (trunk(x_prompt), trunk(x_sample))
```

```python
import functools
import math

import jax
import jax.numpy as jnp
import numpy as np
from jax import lax
from jax.experimental import pallas as pl
from jax.experimental.pallas import tpu as pltpu

F32 = jnp.float32
BF16 = jnp.bfloat16

D_MODEL = 1024
DEPTH = 2
GRID_W = 64
CHUNK = 128
A_WIDTH = 512
A_GROUPS = 4
NA_HEADS = 8
NA_HEAD_DIM = 64
NA_WIDTH = 512
NA_KH = 8
NA_KW = 16
DIFF_HEADS = 8
DIFF_HEAD_DIM = 64
D_FF = 4 * D_MODEL
LN_EPS = 1e-5
ALPHA = (2 * DEPTH) ** 0.25
LAM_INIT_1 = 0.8 - 0.6 * math.exp(-0.3 * 1)
NEG = -1e30
LAG_MARGIN = 60.0

LANES = 128
BF16_EXACT_INT = 256
VMEM_LIMIT = 56 * 1024 * 1024

TM_PROJ0 = 512
TM_POST = 1024
SUB_POST = 256
TK_DIFF = 512
QBLOCKS_DIFF = 4
NA_ROWS_PER_STEP = 8
FF_CHUNK = 1024


def _const_spec(shape):
    nd = len(shape)
    return pl.BlockSpec(shape, lambda *_: (0,) * nd, pipeline_mode=pl.Buffered(1))


def _layer_norm(y, g, b):
    mu = jnp.mean(y, axis=-1, keepdims=True)
    yc = y - mu
    var = jnp.mean(yc * yc, axis=-1, keepdims=True)
    return yc * lax.rsqrt(var + LN_EPS) * g + b


def _dot(a, b):
    return jnp.dot(a, b, preferred_element_type=F32)


def _dot_nt(a, b):
    return lax.dot_general(a, b, (((1,), (1,)), ((), ())), preferred_element_type=F32)


def _proj0_kernel(x_ref, w_ref, lng_ref, lnb_ref, ws_ref, bs_ref, oa_ref, q_ref, k_ref, v_ref):
    tm = x_ref.shape[0]
    xb = x_ref[...].astype(BF16)

    def mm(lo, hi):
        return _dot(xb, w_ref[:, lo:hi])

    u = jax.nn.gelu(mm(0, A_WIDTH))
    v = jax.nn.gelu(mm(A_WIDTH, 2 * A_WIDTH))
    vn = _layer_norm(v, lng_ref[...], lnb_ref[...]).astype(BF16)
    for c in range(tm // CHUNK):
        rs = slice(c * CHUNK, (c + 1) * CHUNK)
        for g in range(A_GROUPS):
            cs = slice(g * LANES, (g + 1) * LANES)
            mixed = _dot(ws_ref[g], vn[rs, cs]) + bs_ref[g]
            oa_ref[rs, cs] = (u[rs, cs] * mixed).astype(BF16)
    base = 2 * A_WIDTH
    q_ref[...] = (mm(base, base + NA_WIDTH) * (NA_HEAD_DIM ** -0.5)).astype(BF16)
    k_ref[...] = mm(base + NA_WIDTH, base + 2 * NA_WIDTH).astype(BF16)
    v_ref[...] = mm(base + 2 * NA_WIDTH, base + 3 * NA_WIDTH).astype(BF16)


def _proj0(x2d, w_in, ln_g, ln_b, w_s, b_s):
    n = x2d.shape[0]
    tm = TM_PROJ0
    row = lambda width: pl.BlockSpec((tm, width), lambda i: (i, 0))
    out = jax.ShapeDtypeStruct((n, A_WIDTH), BF16)
    return pl.pallas_call(
        _proj0_kernel,
        grid=(n // tm,),
        in_specs=[row(D_MODEL), _const_spec(w_in.shape), _const_spec(ln_g.shape), _const_spec(ln_b.shape),
                  _const_spec(w_s.shape), _const_spec(b_s.shape)],
        out_specs=[row(A_WIDTH)] * 4,
        out_shape=[out] * 4,
        compiler_params=pltpu.CompilerParams(dimension_semantics=("arbitrary",), vmem_limit_bytes=VMEM_LIMIT),
        name="proj0_sgu",
    )(x2d, w_in, ln_g, ln_b, w_s, b_s)


def _natten_kernel(q_ref, k_ref, v_ref, bias_ref, o_ref, *, rows, rb):
    i = pl.program_id(1)
    lane = lax.broadcasted_iota(jnp.int32, (GRID_W, LANES), 1)
    lo = lane < NA_HEAD_DIM
    nkeys = NA_KH * GRID_W
    ones = jnp.ones((nkeys, LANES), BF16)
    for rr in range(rb):
        r = i * rb + rr
        r_start = jnp.clip(r - NA_KH // 2, 0, rows - NA_KH)
        d = r - r_start
        kbase = pl.multiple_of(r_start * GRID_W, GRID_W)
        qrows = slice(rr * GRID_W, (rr + 1) * GRID_W)
        for hp in range(NA_HEADS // 2):
            cs = slice(hp * LANES, (hp + 1) * LANES)
            qp = q_ref[qrows, cs]
            z = jnp.zeros_like(qp)
            qq = jnp.concatenate([jnp.where(lo, qp, z), jnp.where(lo, z, qp)], axis=0)
            kk = k_ref[pl.ds(kbase, nkeys), cs]
            vv = jnp.concatenate([v_ref[pl.ds(kbase, nkeys), cs], ones], axis=1)
            s = _dot_nt(qq, kk) + bias_ref[d, hp]
            m = jnp.max(s, axis=-1, keepdims=True)
            pv = _dot(jnp.exp(s - m).astype(BF16), vv)
            pv = pv[:, :LANES] / pv[:, LANES:]
            o_ref[qrows, cs] = jnp.where(lo, pv[:GRID_W], pv[GRID_W:]).astype(BF16)


def _natten(q, k, v, bias, batch, seq):
    rows = seq // GRID_W
    rb = NA_ROWS_PER_STEP
    nblk = rows // rb
    kern = functools.partial(_natten_kernel, rows=rows, rb=rb)
    qspec = pl.BlockSpec((rb * GRID_W, NA_WIDTH), lambda b, i: (b * nblk + i, 0))
    kvspec = pl.BlockSpec((seq, NA_WIDTH), lambda b, i: (b, 0))
    return pl.pallas_call(
        kern,
        grid=(batch, nblk),
        in_specs=[qspec, kvspec, kvspec, _const_spec(bias.shape)],
        out_specs=qspec,
        out_shape=jax.ShapeDtypeStruct(q.shape, BF16),
        compiler_params=pltpu.CompilerParams(dimension_semantics=("arbitrary", "arbitrary"),
                                             vmem_limit_bytes=VMEM_LIMIT),
        name="natten",
    )(q, k, v, bias)


def _na_bias_table(rpb):
    c = np.arange(GRID_W)
    kc = np.arange(GRID_W)
    c_start = np.clip(c - NA_KW // 2, 0, GRID_W - NA_KW)
    valid = (kc[None, :] >= c_start[:, None]) & (kc[None, :] < c_start[:, None] + NA_KW)
    coff = kc[None, :] - c[:, None] + (NA_KW - 1)
    sel = (valid[None] & (coff[None] == np.arange(2 * NA_KW - 1)[:, None, None])).astype(np.float32)
    t = jnp.einsum("hro,ock->hrck", rpb, jnp.asarray(sel), precision=lax.Precision.HIGHEST)
    t = t + jnp.asarray(np.where(valid, 0.0, NEG).astype(np.float32))[None, None]
    t = jnp.stack([t[:, NA_KH - 1 - d:2 * NA_KH - 1 - d] for d in range(NA_KH)])
    t = jnp.transpose(t, (0, 1, 3, 2, 4))
    return t.reshape(NA_KH, NA_HEADS // 2, 2 * GRID_W, NA_KH * GRID_W).astype(F32)


def _post_kernel(*refs, n_in, sub):
    a_refs = refs[:n_in]
    wo_ref, x_ref, g1_ref, b1_ref, w1_ref, w2_ref, g2_ref, b2_ref, o_ref = refs[n_in:]
    subs = [slice(k * sub, (k + 1) * sub) for k in range(x_ref.shape[0] // sub)]
    x1s = []
    for rs in subs:
        a = jnp.concatenate([a_ref[rs, :] for a_ref in a_refs], axis=1)
        mix = _dot(a, wo_ref[...])
        x1s.append(_layer_norm(ALPHA * x_ref[rs, :] + mix, g1_ref[...], b1_ref[...]))
    for rs, x1 in zip(subs, x1s):
        x1b = x1.astype(BF16)
        acc = None
        for j in range(D_FF // FF_CHUNK):
            fs = slice(j * FF_CHUNK, (j + 1) * FF_CHUNK)
            hid = jnp.square(jnp.maximum(_dot(x1b, w1_ref[:, fs]), 0.0)).astype(BF16)
            part = _dot(hid, w2_ref[fs, :])
            acc = part if acc is None else acc + part
        o_ref[rs, :] = _layer_norm(ALPHA * x1 + acc, g2_ref[...], b2_ref[...])


def _post(acts, w_out, x2d, g1, b1, w1, w2, g2, b2):
    n = x2d.shape[0]
    tm = TM_POST
    n_in = len(acts)
    row = lambda width: pl.BlockSpec((tm, width), lambda i: (i, 0))
    in_specs = ([row(a.shape[1]) for a in acts] + [_const_spec(w_out.shape), row(D_MODEL)]
                + [_const_spec(p.shape) for p in (g1, b1, w1, w2, g2, b2)])
    return pl.pallas_call(
        functools.partial(_post_kernel, n_in=n_in, sub=SUB_POST),
        grid=(n // tm,),
        in_specs=in_specs,
        out_specs=row(D_MODEL),
        out_shape=jax.ShapeDtypeStruct((n, D_MODEL), F32),
        compiler_params=pltpu.CompilerParams(dimension_semantics=("arbitrary",), vmem_limit_bytes=VMEM_LIMIT),
        name="post",
    )(*acts, w_out, x2d, g1, b1, w1, w2, g2, b2)


def _proj1_kernel(x_ref, wqt_ref, wk_ref, wvt_ref, qt_ref, k_ref, vt_ref):
    xb = x_ref[...].astype(BF16)
    qt_ref[0] = (_dot_nt(wqt_ref[...], xb) * (DIFF_HEAD_DIM ** -0.5)).astype(BF16)
    k_ref[...] = _dot(xb, wk_ref[...]).astype(BF16)
    vt_ref[0] = _dot_nt(wvt_ref[...], xb).astype(BF16)


def _proj1(x2d, wqt, wk, wvt):
    n = x2d.shape[0]
    tm = TK_DIFF
    row = pl.BlockSpec((tm, D_MODEL), lambda i: (i, 0))
    slab = pl.BlockSpec((1, D_MODEL, tm), lambda i: (i, 0, 0))
    slabs = jax.ShapeDtypeStruct((n // tm, D_MODEL, tm), BF16)
    return pl.pallas_call(
        _proj1_kernel,
        grid=(n // tm,),
        in_specs=[row, _const_spec(wqt.shape), _const_spec(wk.shape), _const_spec(wvt.shape)],
        out_specs=[slab, row, slab],
        out_shape=[slabs, jax.ShapeDtypeStruct((n, D_MODEL), BF16), slabs],
        compiler_params=pltpu.CompilerParams(dimension_semantics=("arbitrary",), vmem_limit_bytes=VMEM_LIMIT),
        name="proj1",
    )(x2d, wqt, wk, wvt)


def _diff_kernel(lam_ref, slope_ref, qt_ref, k_ref, vt_ref, feat_ref, g_ref, o_ref, c_ref, *, t, nk, nqb):
    h = pl.program_id(1)
    step = pl.program_id(2)
    slope = slope_ref[h]
    lam = lam_ref[0]
    hd = 2 * DIFF_HEAD_DIM
    ones_rows = 16

    cio = lax.broadcasted_iota(jnp.int32, (1, 2 * t), 1)
    qloc = jnp.where(cio >= t, cio - t, cio)

    @pl.when(step == 0)
    def _():
        kloc = lax.broadcasted_iota(jnp.int32, (t, 2 * t), 0)
        c_ref[...] = (-2.0 * slope) * jnp.maximum(kloc - qloc, 0).astype(F32)

    top = lax.broadcasted_iota(jnp.int32, (hd, t), 0) < DIFF_HEAD_DIM
    qf = jnp.where(lax.broadcasted_iota(jnp.int32, (hd, 2 * t), 0) < 2, slope, 0.0).astype(BF16)
    ones = jnp.ones((ones_rows, t), BF16)

    def block(qb):
        i = step * nqb + qb
        qt = qt_ref[qb]
        z = jnp.zeros_like(qt)
        qm = jnp.concatenate([jnp.where(top, qt, z), jnp.where(top, z, qt)], axis=1)
        qa = jnp.concatenate([qm, qf], axis=0)

        def tile_index(jj):
            return jnp.where(jj == 0, i, (i + jj) & (nk - 1))

        def scores(jj):
            j = tile_index(jj)
            above = j > i
            kt = k_ref[pl.ds(pl.multiple_of(j * t, t), t), :]
            ka = jnp.concatenate([kt, feat_ref[jnp.where(above, 1, 0)]], axis=1)
            s = _dot(ka, qa)
            if isinstance(jj, int):
                s = s + c_ref[...] if jj == 0 else s
            else:
                s = s + jnp.where(jj == 0, 1.0, 0.0) * c_ref[...]
            cpos = slope * ((j - i) * t - qloc).astype(F32)
            cvec = jnp.where(above, -cpos, cpos)
            return s, cvec

        def weighted_values(jj, s, ref):
            vta = jnp.concatenate([vt_ref[tile_index(jj)], ones], axis=0)
            return _dot(vta, jnp.exp(s - ref).astype(BF16))

        def lagged_pass():
            s, cvec = scores(0)
            m = jnp.max(s, axis=0, keepdims=True) + cvec
            acc = weighted_values(0, s, m - cvec)
            excess = jnp.zeros_like(m)
            for jj in range(1, nk):
                s, cvec = scores(jj)
                r = weighted_values(jj, s, m - cvec)
                smax = jnp.max(s, axis=0, keepdims=True) + cvec
                m_new = jnp.maximum(m, smax)
                excess = jnp.maximum(excess, smax - m)
                acc = jnp.exp(m - m_new) * (acc + r)
                m = m_new
            return acc, excess

        def per_tile_max_pass():
            def tile(jj, carry):
                m, acc = carry
                s, cvec = scores(jj)
                m_new = jnp.maximum(m, jnp.max(s, axis=0, keepdims=True) + cvec)
                return m_new, jnp.exp(m - m_new) * acc + weighted_values(jj, s, m_new - cvec)

            init = (jnp.full((1, 2 * t), -jnp.inf, F32), jnp.zeros((hd + ones_rows, 2 * t), F32))
            return lax.fori_loop(0, nk, tile, init)[1]

        return lagged_pass, per_tile_max_pass

    def finish(qb, acc):
        inv1 = 1.0 / acc[hd:hd + 1, :t]
        inv2 = 1.0 / acc[hd:hd + 1, t:]
        o = acc[:hd, :t] * inv1 - lam * (acc[:hd, t:] * inv2)
        ms = jnp.mean(o * o, axis=0, keepdims=True)
        o = o * lax.rsqrt(ms + LN_EPS) * g_ref[...] * (1.0 - LAM_INIT_1)
        o_ref[pl.ds(pl.multiple_of(qb * t, t), t), :] = o.T.astype(BF16)

    ok = None
    for qb in range(nqb):
        acc, excess = block(qb)[0]()
        finish(qb, acc)
        good = jnp.logical_and(jnp.sum(acc * 0.0) == 0.0, jnp.max(excess) <= LAG_MARGIN)
        ok = good if ok is None else jnp.logical_and(ok, good)

    @pl.when(jnp.logical_not(ok))
    def _():
        @pl.loop(0, nqb)
        def _(qb):
            finish(qb, block(qb)[1]())


def _diff_features(t):
    a = jnp.arange(t, dtype=jnp.int32)
    lo = a % BF16_EXACT_INT
    f = jnp.zeros((t, 2 * DIFF_HEAD_DIM), F32).at[:, 0].set(lo.astype(F32)).at[:, 1].set((a - lo).astype(F32))
    return jnp.stack([f, -f]).astype(BF16)


def _diffattn(lam, slopes, qt, k, vt, g_col, batch, seq):
    t = TK_DIFF
    nk = seq // t
    assert nk & (nk - 1) == 0
    hd = 2 * DIFF_HEAD_DIM
    feat = _diff_features(t)
    smem = pl.BlockSpec(memory_space=pltpu.SMEM)
    nqb = QBLOCKS_DIFF
    steps = nk // nqb
    return pl.pallas_call(
        functools.partial(_diff_kernel, t=t, nk=nk, nqb=nqb),
        grid=(batch, DIFF_HEADS, steps),
        in_specs=[smem, smem,
                  pl.BlockSpec((nqb, hd, t), lambda b, h, i: (b * steps + i, h, 0)),
                  pl.BlockSpec((seq, hd), lambda b, h, i: (b, h)),
                  pl.BlockSpec((nk, hd, t), lambda b, h, i: (b, h, 0)),
                  _const_spec(feat.shape),
                  _const_spec(g_col.shape)],
        out_specs=pl.BlockSpec((nqb * t, hd), lambda b, h, i: (b * steps + i, h)),
        out_shape=jax.ShapeDtypeStruct(k.shape, BF16),
        scratch_shapes=[pltpu.VMEM((t, 2 * t), F32)],
        compiler_params=pltpu.CompilerParams(dimension_semantics=("arbitrary",) * 3,
                                             vmem_limit_bytes=VMEM_LIMIT),
        name="diffattn",
    )(lam, slopes, qt, k, vt, feat, g_col)


def kernel(x_prompt, x_sample, l0_w_in, l0_w_out, l0_gate_ln_g, l0_gate_ln_b, l0_w_spatial, l0_b_spatial, l0_na_rpb, l0_ln1_g, l0_ln1_b, l0_w_ff1, l0_w_ff2, l0_ln2_g, l0_ln2_b, l1_w_in, l1_w_out, l1_lambda_q1, l1_lambda_k1, l1_lambda_q2, l1_lambda_k2, l1_subln_g, l1_ln1_g, l1_ln1_b, l1_w_ff1, l1_w_ff2, l1_ln2_g, l1_ln2_b):
    row = lambda p: p.reshape(1, -1).astype(F32)
    w_in0 = l0_w_in.astype(BF16)
    w_out0 = l0_w_out.astype(BF16)
    w_s = l0_w_spatial.astype(BF16)
    b_s = jnp.broadcast_to(l0_b_spatial.astype(F32)[:, :, None], (A_GROUPS, CHUNK, LANES))
    na_bias = _na_bias_table(l0_na_rpb.astype(F32))
    wqt1 = l1_w_in[:, :D_MODEL].T.astype(BF16)
    wk1 = l1_w_in[:, D_MODEL:2 * D_MODEL].astype(BF16)
    wvt1 = l1_w_in[:, 2 * D_MODEL:].T.astype(BF16)
    w_out1 = l1_w_out.astype(BF16)
    lam1 = jnp.exp(jnp.sum((l1_lambda_q1 * l1_lambda_k1).astype(F32)))
    lam2 = jnp.exp(jnp.sum((l1_lambda_q2 * l1_lambda_k2).astype(F32)))
    lam = (lam1 - lam2 + LAM_INIT_1).reshape(1).astype(F32)
    slopes = jnp.exp2(-8.0 * jnp.arange(1, DIFF_HEADS + 1, dtype=F32) / DIFF_HEADS)
    g_col = l1_subln_g.astype(F32).reshape(-1, 1)
    ff0 = (l0_w_ff1.astype(BF16), l0_w_ff2.astype(BF16))
    ff1 = (l1_w_ff1.astype(BF16), l1_w_ff2.astype(BF16))

    def trunk(x):
        batch, seq, _ = x.shape
        x2d = x.reshape(batch * seq, D_MODEL)
        out_a, q, k, v = _proj0(x2d, w_in0, row(l0_gate_ln_g), row(l0_gate_ln_b), w_s, b_s)
        out_b = _natten(q, k, v, na_bias, batch, seq)
        x2d = _post([out_a, out_b], w_out0, x2d, row(l0_ln1_g), row(l0_ln1_b),
                    ff0[0], ff0[1], row(l0_ln2_g), row(l0_ln2_b))
        qt, k, vt = _proj1(x2d, wqt1, wk1, wvt1)
        o = _diffattn(lam, slopes, qt, k, vt, g_col, batch, seq)
        x2d = _post([o], w_out1, x2d, row(l1_ln1_g), row(l1_ln1_b),
                    ff1[0], ff1[1], row(l1_ln2_g), row(l1_ln2_b))
        return x2d.reshape(batch, seq, D_MODEL)

    return (trunk(x_prompt), trunk(x_sample))
```

```python
import functools
import math

import jax
import jax.numpy as jnp
import numpy as np
from jax import lax
from jax.experimental import pallas as pl
from jax.experimental.pallas import tpu as pltpu

F32 = jnp.float32
BF16 = jnp.bfloat16

D_MODEL = 1024
DEPTH = 2
GRID_W = 64
CHUNK = 128
A_WIDTH = 512
A_GROUPS = 4
NA_HEADS = 8
NA_HEAD_DIM = 64
NA_WIDTH = 512
NA_KH = 8
NA_KW = 16
DIFF_HEADS = 8
DIFF_HEAD_DIM = 64
D_FF = 4 * D_MODEL
LN_EPS = 1e-5
ALPHA = (2 * DEPTH) ** 0.25
LAM_INIT_1 = 0.8 - 0.6 * math.exp(-0.3 * 1)
NEG = -1e30
LAG_MARGIN = 60.0
LOG2E = float(np.float32(1.0 / math.log(2.0)))

LANES = 128
BF16_EXACT_INT = 256
VMEM_LIMIT = 56 * 1024 * 1024

TM_PROJ0 = 512
TM_POST = 1024
SUB_POST = 256
TK_DIFF = 512
QBLOCKS_DIFF = 4
NA_ROWS_PER_STEP = 8
FF_CHUNK = 1024


def _const_spec(shape):
    nd = len(shape)
    return pl.BlockSpec(shape, lambda *_: (0,) * nd, pipeline_mode=pl.Buffered(1))


def _layer_norm(y, g, b):
    mu = jnp.mean(y, axis=-1, keepdims=True)
    yc = y - mu
    var = jnp.mean(yc * yc, axis=-1, keepdims=True)
    return yc * lax.rsqrt(var + LN_EPS) * g + b


def _dot(a, b):
    return jnp.dot(a, b, preferred_element_type=F32)


def _dot_nt(a, b):
    return lax.dot_general(a, b, (((1,), (1,)), ((), ())), preferred_element_type=F32)


def _proj0_kernel(x_ref, w_ref, lng_ref, lnb_ref, ws_ref, bs_ref, oa_ref, q_ref, k_ref, v_ref):
    tm = x_ref.shape[0]
    xb = x_ref[...].astype(BF16)

    def mm(lo, hi):
        return _dot(xb, w_ref[:, lo:hi])

    u = jax.nn.gelu(mm(0, A_WIDTH))
    v = jax.nn.gelu(mm(A_WIDTH, 2 * A_WIDTH))
    vn = _layer_norm(v, lng_ref[...], lnb_ref[...]).astype(BF16)
    for c in range(tm // CHUNK):
        rs = slice(c * CHUNK, (c + 1) * CHUNK)
        for g in range(A_GROUPS):
            cs = slice(g * LANES, (g + 1) * LANES)
            mixed = _dot(ws_ref[g], vn[rs, cs]) + bs_ref[g]
            oa_ref[rs, cs] = (u[rs, cs] * mixed).astype(BF16)
    base = 2 * A_WIDTH
    q_ref[...] = (mm(base, base + NA_WIDTH) * (NA_HEAD_DIM ** -0.5)).astype(BF16)
    k_ref[...] = mm(base + NA_WIDTH, base + 2 * NA_WIDTH).astype(BF16)
    v_ref[...] = mm(base + 2 * NA_WIDTH, base + 3 * NA_WIDTH).astype(BF16)


def _proj0(x2d, w_in, ln_g, ln_b, w_s, b_s):
    n = x2d.shape[0]
    tm = TM_PROJ0
    row = lambda width: pl.BlockSpec((tm, width), lambda i: (i, 0))
    out = jax.ShapeDtypeStruct((n, A_WIDTH), BF16)
    return pl.pallas_call(
        _proj0_kernel,
        grid=(n // tm,),
        in_specs=[row(D_MODEL), _const_spec(w_in.shape), _const_spec(ln_g.shape), _const_spec(ln_b.shape),
                  _const_spec(w_s.shape), _const_spec(b_s.shape)],
        out_specs=[row(A_WIDTH)] * 4,
        out_shape=[out] * 4,
        compiler_params=pltpu.CompilerParams(dimension_semantics=("arbitrary",), vmem_limit_bytes=VMEM_LIMIT),
        name="proj0_sgu",
    )(x2d, w_in, ln_g, ln_b, w_s, b_s)


def _natten_kernel(q_ref, k_ref, v_ref, bias_ref, o_ref, *, rows, rb):
    i = pl.program_id(1)
    lane = lax.broadcasted_iota(jnp.int32, (GRID_W, LANES), 1)
    lo = lane < NA_HEAD_DIM
    nkeys = NA_KH * GRID_W
    ones = jnp.ones((nkeys, LANES), BF16)
    for rr in range(rb):
        r = i * rb + rr
        r_start = jnp.clip(r - NA_KH // 2, 0, rows - NA_KH)
        d = r - r_start
        kbase = pl.multiple_of(r_start * GRID_W, GRID_W)
        qrows = slice(rr * GRID_W, (rr + 1) * GRID_W)
        for hp in range(NA_HEADS // 2):
            cs = slice(hp * LANES, (hp + 1) * LANES)
            qp = q_ref[qrows, cs]
            z = jnp.zeros_like(qp)
            qq = jnp.concatenate([jnp.where(lo, qp, z), jnp.where(lo, z, qp)], axis=0)
            kk = k_ref[pl.ds(kbase, nkeys), cs]
            vv = jnp.concatenate([v_ref[pl.ds(kbase, nkeys), cs], ones], axis=1)
            s = _dot_nt(qq, kk) + bias_ref[d, hp]
            m = jnp.max(s, axis=-1, keepdims=True)
            pv = _dot(jnp.exp(s - m).astype(BF16), vv)
            pv = pv[:, :LANES] / pv[:, LANES:]
            o_ref[qrows, cs] = jnp.where(lo, pv[:GRID_W], pv[GRID_W:]).astype(BF16)


def _natten(q, k, v, bias, batch, seq):
    rows = seq // GRID_W
    rb = NA_ROWS_PER_STEP
    nblk = rows // rb
    kern = functools.partial(_natten_kernel, rows=rows, rb=rb)
    qspec = pl.BlockSpec((rb * GRID_W, NA_WIDTH), lambda b, i: (b * nblk + i, 0))
    kvspec = pl.BlockSpec((seq, NA_WIDTH), lambda b, i: (b, 0))
    return pl.pallas_call(
        kern,
        grid=(batch, nblk),
        in_specs=[qspec, kvspec, kvspec, _const_spec(bias.shape)],
        out_specs=qspec,
        out_shape=jax.ShapeDtypeStruct(q.shape, BF16),
        compiler_params=pltpu.CompilerParams(dimension_semantics=("arbitrary", "arbitrary"),
                                             vmem_limit_bytes=VMEM_LIMIT),
        name="natten",
    )(q, k, v, bias)


def _na_bias_table(rpb):
    c = np.arange(GRID_W)
    kc = np.arange(GRID_W)
    c_start = np.clip(c - NA_KW // 2, 0, GRID_W - NA_KW)
    valid = (kc[None, :] >= c_start[:, None]) & (kc[None, :] < c_start[:, None] + NA_KW)
    coff = kc[None, :] - c[:, None] + (NA_KW - 1)
    sel = (valid[None] & (coff[None] == np.arange(2 * NA_KW - 1)[:, None, None])).astype(np.float32)
    t = jnp.einsum("hro,ock->hcrk", rpb, jnp.asarray(sel), precision=lax.Precision.HIGHEST)
    t = t + jnp.asarray(np.where(valid, 0.0, NEG).astype(np.float32))[None, :, None, :]
    t = jnp.stack([t[:, :, NA_KH - 1 - d:2 * NA_KH - 1 - d] for d in range(NA_KH)])
    return t.reshape(NA_KH, NA_HEADS // 2, 2 * GRID_W, NA_KH * GRID_W).astype(F32)


def _post_kernel(*refs, n_in, sub):
    a_refs = refs[:n_in]
    wo_ref, x_ref, g1_ref, b1_ref, w1_ref, w2_ref, g2_ref, b2_ref, o_ref = refs[n_in:]
    subs = [slice(k * sub, (k + 1) * sub) for k in range(x_ref.shape[0] // sub)]
    x1s = []
    for rs in subs:
        a = jnp.concatenate([a_ref[rs, :] for a_ref in a_refs], axis=1)
        mix = _dot(a, wo_ref[...])
        x1s.append(_layer_norm(ALPHA * x_ref[rs, :] + mix, g1_ref[...], b1_ref[...]))
    for rs, x1 in zip(subs, x1s):
        x1b = x1.astype(BF16)
        acc = None
        for j in range(D_FF // FF_CHUNK):
            fs = slice(j * FF_CHUNK, (j + 1) * FF_CHUNK)
            hid = jnp.square(jnp.maximum(_dot(x1b, w1_ref[:, fs]), 0.0)).astype(BF16)
            part = _dot(hid, w2_ref[fs, :])
            acc = part if acc is None else acc + part
        o_ref[rs, :] = _layer_norm(ALPHA * x1 + acc, g2_ref[...], b2_ref[...])


def _post(acts, w_out, x2d, g1, b1, w1, w2, g2, b2):
    n = x2d.shape[0]
    tm = TM_POST
    n_in = len(acts)
    row = lambda width: pl.BlockSpec((tm, width), lambda i: (i, 0))
    in_specs = ([row(a.shape[1]) for a in acts] + [_const_spec(w_out.shape), row(D_MODEL)]
                + [_const_spec(p.shape) for p in (g1, b1, w1, w2, g2, b2)])
    return pl.pallas_call(
        functools.partial(_post_kernel, n_in=n_in, sub=SUB_POST),
        grid=(n // tm,),
        in_specs=in_specs,
        out_specs=row(D_MODEL),
        out_shape=jax.ShapeDtypeStruct((n, D_MODEL), F32),
        compiler_params=pltpu.CompilerParams(dimension_semantics=("arbitrary",), vmem_limit_bytes=VMEM_LIMIT),
        name="post",
    )(*acts, w_out, x2d, g1, b1, w1, w2, g2, b2)


def _proj1_kernel(x_ref, wqt_ref, wk_ref, wvt_ref, qt_ref, k_ref, vt_ref):
    xb = x_ref[...].astype(BF16)
    qt_ref[0] = (_dot_nt(wqt_ref[...], xb) * (DIFF_HEAD_DIM ** -0.5 * LOG2E)).astype(BF16)
    k_ref[...] = _dot(xb, wk_ref[...]).astype(BF16)
    vt_ref[0] = _dot_nt(wvt_ref[...], xb).astype(BF16)


def _proj1(x2d, wqt, wk, wvt):
    n = x2d.shape[0]
    tm = TK_DIFF
    row = pl.BlockSpec((tm, D_MODEL), lambda i: (i, 0))
    slab = pl.BlockSpec((1, D_MODEL, tm), lambda i: (i, 0, 0))
    slabs = jax.ShapeDtypeStruct((n // tm, D_MODEL, tm), BF16)
    return pl.pallas_call(
        _proj1_kernel,
        grid=(n // tm,),
        in_specs=[row, _const_spec(wqt.shape), _const_spec(wk.shape), _const_spec(wvt.shape)],
        out_specs=[slab, row, slab],
        out_shape=[slabs, jax.ShapeDtypeStruct((n, D_MODEL), BF16), slabs],
        compiler_params=pltpu.CompilerParams(dimension_semantics=("arbitrary",), vmem_limit_bytes=VMEM_LIMIT),
        name="proj1",
    )(x2d, wqt, wk, wvt)


def _diff_kernel(lam_ref, slope_ref, qt_ref, k_ref, vt_ref, feat_ref, g_ref, o_ref, c_ref, *, t, nk, nqb):
    h = pl.program_id(1)
    step = pl.program_id(2)
    slope = slope_ref[h] * LOG2E
    lam = lam_ref[0]
    hd = 2 * DIFF_HEAD_DIM
    ones_rows = 16

    cio = lax.broadcasted_iota(jnp.int32, (1, 2 * t), 1)
    qloc = jnp.where(cio >= t, cio - t, cio)

    @pl.when(step == 0)
    def _():
        kloc = lax.broadcasted_iota(jnp.int32, (t, 2 * t), 0)
        c_ref[...] = (-2.0 * slope) * jnp.maximum(kloc - qloc, 0).astype(F32)

    top = lax.broadcasted_iota(jnp.int32, (hd, t), 0) < DIFF_HEAD_DIM
    frow = lax.broadcasted_iota(jnp.int32, (hd, 2 * t), 0)
    rest = jnp.full((hd, 2 * t), slope, F32)
    qf = jnp.zeros((hd, 2 * t), F32)
    for term in range(3):
        part = rest.astype(BF16).astype(F32)
        qf = jnp.where((frow >> 1) == term, part, qf)
        rest = rest - part
    qf = qf.astype(BF16)
    ones = jnp.ones((ones_rows, t), BF16)

    def block(qb):
        i = step * nqb + qb
        qt = qt_ref[qb]
        z = jnp.zeros_like(qt)
        qm = jnp.concatenate([jnp.where(top, qt, z), jnp.where(top, z, qt)], axis=1)
        qa = jnp.concatenate([qm, qf], axis=0)

        def tile_index(jj):
            return jnp.where(jj == 0, i, (i + jj) & (nk - 1))

        def scores(jj):
            j = tile_index(jj)
            above = j > i
            kt = k_ref[pl.ds(pl.multiple_of(j * t, t), t), :]
            ka = jnp.concatenate([kt, feat_ref[jnp.where(above, 1, 0)]], axis=1)
            s = _dot(ka, qa)
            if isinstance(jj, int):
                s = s + c_ref[...] if jj == 0 else s
            else:
                s = s + jnp.where(jj == 0, 1.0, 0.0) * c_ref[...]
            cpos = slope * ((j - i) * t - qloc).astype(F32)
            cvec = jnp.where(above, -cpos, cpos)
            return s, cvec

        def weighted_values(jj, s, ref):
            vta = jnp.concatenate([vt_ref[tile_index(jj)], ones], axis=0)
            return _dot(vta, jnp.exp2(s - ref).astype(BF16))

        def lagged_pass():
            s, cvec = scores(0)
            m = jnp.max(s, axis=0, keepdims=True) + cvec
            acc = weighted_values(0, s, m - cvec)
            excess = jnp.zeros_like(m)
            for jj in range(1, nk):
                s, cvec = scores(jj)
                r = weighted_values(jj, s, m - cvec)
                smax = jnp.max(s, axis=0, keepdims=True) + cvec
                m_new = jnp.maximum(m, smax)
                excess = jnp.maximum(excess, smax - m)
                acc = jnp.exp2(m - m_new) * (acc + r)
                m = m_new
            return acc, excess

        def per_tile_max_pass():
            def tile(jj, carry):
                m, acc = carry
                s, cvec = scores(jj)
                m_new = jnp.maximum(m, jnp.max(s, axis=0, keepdims=True) + cvec)
                return m_new, jnp.exp2(m - m_new) * acc + weighted_values(jj, s, m_new - cvec)

            init = (jnp.full((1, 2 * t), -jnp.inf, F32), jnp.zeros((hd + ones_rows, 2 * t), F32))
            return lax.fori_loop(0, nk, tile, init)[1]

        return lagged_pass, per_tile_max_pass

    def finish(qb, acc):
        inv1 = 1.0 / acc[hd:hd + 1, :t]
        inv2 = 1.0 / acc[hd:hd + 1, t:]
        o = acc[:hd, :t] * inv1 - lam * (acc[:hd, t:] * inv2)
        ms = jnp.mean(o * o, axis=0, keepdims=True)
        o = o * lax.rsqrt(ms + LN_EPS) * g_ref[...] * (1.0 - LAM_INIT_1)
        o_ref[pl.ds(pl.multiple_of(qb * t, t), t), :] = o.T.astype(BF16)

    ok = None
    for qb in range(nqb):
        acc, excess = block(qb)[0]()
        finish(qb, acc)
        good = jnp.logical_and(jnp.sum(acc * 0.0) == 0.0, jnp.max(excess) <= LAG_MARGIN * LOG2E)
        ok = good if ok is None else jnp.logical_and(ok, good)

    @pl.when(jnp.logical_not(ok))
    def _():
        @pl.loop(0, nqb)
        def _(qb):
            finish(qb, block(qb)[1]())


def _diff_features(t):
    a = np.arange(t)
    lo = a % BF16_EXACT_INT
    f = np.zeros((t, 2 * DIFF_HEAD_DIM), np.float32)
    f[:, 0:6:2] = lo[:, None]
    f[:, 1:6:2] = (a - lo)[:, None]
    return jnp.asarray(np.stack([f, -f]), BF16)


def _diffattn(lam, slopes, qt, k, vt, g_col, batch, seq):
    t = TK_DIFF
    nk = seq // t
    assert nk & (nk - 1) == 0
    hd = 2 * DIFF_HEAD_DIM
    feat = _diff_features(t)
    smem = pl.BlockSpec(memory_space=pltpu.SMEM)
    nqb = QBLOCKS_DIFF
    steps = nk // nqb
    return pl.pallas_call(
        functools.partial(_diff_kernel, t=t, nk=nk, nqb=nqb),
        grid=(batch, DIFF_HEADS, steps),
        in_specs=[smem, smem,
                  pl.BlockSpec((nqb, hd, t), lambda b, h, i: (b * steps + i, h, 0)),
                  pl.BlockSpec((seq, hd), lambda b, h, i: (b, h)),
                  pl.BlockSpec((nk, hd, t), lambda b, h, i: (b, h, 0)),
                  _const_spec(feat.shape),
                  _const_spec(g_col.shape)],
        out_specs=pl.BlockSpec((nqb * t, hd), lambda b, h, i: (b * steps + i, h)),
        out_shape=jax.ShapeDtypeStruct(k.shape, BF16),
        scratch_shapes=[pltpu.VMEM((t, 2 * t), F32)],
        compiler_params=pltpu.CompilerParams(dimension_semantics=("arbitrary",) * 3,
                                             vmem_limit_bytes=VMEM_LIMIT),
        name="diffattn",
    )(lam, slopes, qt, k, vt, feat, g_col)


def kernel(x_prompt, x_sample, l0_w_in, l0_w_out, l0_gate_ln_g, l0_gate_ln_b, l0_w_spatial, l0_b_spatial, l0_na_rpb, l0_ln1_g, l0_ln1_b, l0_w_ff1, l0_w_ff2, l0_ln2_g, l0_ln2_b, l1_w_in, l1_w_out, l1_lambda_q1, l1_lambda_k1, l1_lambda_q2, l1_lambda_k2, l1_subln_g, l1_ln1_g, l1_ln1_b, l1_w_ff1, l1_w_ff2, l1_ln2_g, l1_ln2_b):
    row = lambda p: p.reshape(1, -1).astype(F32)
    w_in0 = l0_w_in.astype(BF16)
    w_out0 = l0_w_out.astype(BF16)
    w_s = l0_w_spatial.astype(BF16)
    b_s = jnp.broadcast_to(l0_b_spatial.astype(F32)[:, :, None], (A_GROUPS, CHUNK, LANES))
    na_bias = _na_bias_table(l0_na_rpb.astype(F32))
    wqt1 = l1_w_in[:, :D_MODEL].T.astype(BF16)
    wk1 = l1_w_in[:, D_MODEL:2 * D_MODEL].astype(BF16)
    wvt1 = l1_w_in[:, 2 * D_MODEL:].T.astype(BF16)
    w_out1 = l1_w_out.astype(BF16)
    lam1 = jnp.exp(jnp.sum((l1_lambda_q1 * l1_lambda_k1).astype(F32)))
    lam2 = jnp.exp(jnp.sum((l1_lambda_q2 * l1_lambda_k2).astype(F32)))
    lam = (lam1 - lam2 + LAM_INIT_1).reshape(1).astype(F32)
    slopes = jnp.exp2(-8.0 * jnp.arange(1, DIFF_HEADS + 1, dtype=F32) / DIFF_HEADS)
    g_col = l1_subln_g.astype(F32).reshape(-1, 1)
    ff0 = (l0_w_ff1.astype(BF16), l0_w_ff2.astype(BF16))
    ff1 = (l1_w_ff1.astype(BF16), l1_w_ff2.astype(BF16))

    def trunk(x):
        batch, seq, _ = x.shape
        x2d = x.reshape(batch * seq, D_MODEL)
        out_a, q, k, v = _proj0(x2d, w_in0, row(l0_gate_ln_g), row(l0_gate_ln_b), w_s, b_s)
        out_b = _natten(q, k, v, na_bias, batch, seq)
        x2d = _post([out_a, out_b], w_out0, x2d, row(l0_ln1_g), row(l0_ln1_b),
                    ff0[0], ff0[1], row(l0_ln2_g), row(l0_ln2_b))
        qt, k, vt = _proj1(x2d, wqt1, wk1, wvt1)
        o = _diffattn(lam, slopes, qt, k, vt, g_col, batch, seq)
        x2d = _post([o], w_out1, x2d, row(l1_ln1_g), row(l1_ln1_b),
                    ff1[0], ff1[1], row(l1_ln2_g), row(l1_ln2_b))
        return x2d.reshape(batch, seq, D_MODEL)

    return (trunk(x_prompt), trunk(x_sample))
```

```python
import functools
import math

import jax
import jax.numpy as jnp
import numpy as np
from jax import lax
from jax.experimental import pallas as pl
from jax.experimental.pallas import tpu as pltpu

F32 = jnp.float32
BF16 = jnp.bfloat16

D_MODEL = 1024
DEPTH = 2
GRID_W = 64
CHUNK = 128
A_WIDTH = 512
A_GROUPS = 4
NA_HEADS = 8
NA_HEAD_DIM = 64
NA_WIDTH = 512
NA_KH = 8
NA_KW = 16
DIFF_HEADS = 8
DIFF_HEAD_DIM = 64
D_FF = 4 * D_MODEL
LN_EPS = 1e-5
ALPHA = (2 * DEPTH) ** 0.25
LAM_INIT_1 = 0.8 - 0.6 * math.exp(-0.3 * 1)
NEG = -1e30
LAG_MARGIN = 60.0
LOG2E = float(np.float32(1.0 / math.log(2.0)))

LANES = 128
BF16_EXACT_INT = 256
VMEM_LIMIT = 56 * 1024 * 1024

TM_PROJ0 = 512
TM_POST = 1024
SUB_POST = 256
TK_DIFF = 512
QBLOCKS_DIFF = 4
NA_ROWS_PER_STEP = 8
FF_CHUNK = 1024


def _const_spec(shape):
    nd = len(shape)
    return pl.BlockSpec(shape, lambda *_: (0,) * nd, pipeline_mode=pl.Buffered(1))


def _layer_norm(y, g, b):
    mu = jnp.mean(y, axis=-1, keepdims=True)
    yc = y - mu
    var = jnp.mean(yc * yc, axis=-1, keepdims=True)
    return yc * lax.rsqrt(var + LN_EPS) * g + b


def _dot(a, b):
    return jnp.dot(a, b, preferred_element_type=F32)


def _dot_nt(a, b):
    return lax.dot_general(a, b, (((1,), (1,)), ((), ())), preferred_element_type=F32)


def _proj0_kernel(x_ref, w_ref, lng_ref, lnb_ref, ws_ref, bs_ref, oa_ref, q_ref, k_ref, v_ref):
    tm = x_ref.shape[0]
    xb = x_ref[...].astype(BF16)

    def mm(lo, hi):
        return _dot(xb, w_ref[:, lo:hi])

    u = jax.nn.gelu(mm(0, A_WIDTH))
    v = jax.nn.gelu(mm(A_WIDTH, 2 * A_WIDTH))
    vn = _layer_norm(v, lng_ref[...], lnb_ref[...]).astype(BF16)
    for c in range(tm // CHUNK):
        rs = slice(c * CHUNK, (c + 1) * CHUNK)
        for g in range(A_GROUPS):
            cs = slice(g * LANES, (g + 1) * LANES)
            mixed = _dot(ws_ref[g], vn[rs, cs]) + bs_ref[g]
            oa_ref[rs, cs] = (u[rs, cs] * mixed).astype(BF16)
    base = 2 * A_WIDTH
    q_ref[...] = (mm(base, base + NA_WIDTH) * (NA_HEAD_DIM ** -0.5)).astype(BF16)
    k_ref[...] = mm(base + NA_WIDTH, base + 2 * NA_WIDTH).astype(BF16)
    v_ref[...] = mm(base + 2 * NA_WIDTH, base + 3 * NA_WIDTH).astype(BF16)


def _proj0(x2d, w_in, ln_g, ln_b, w_s, b_s):
    n = x2d.shape[0]
    tm = TM_PROJ0
    row = lambda width: pl.BlockSpec((tm, width), lambda i: (i, 0))
    out = jax.ShapeDtypeStruct((n, A_WIDTH), BF16)
    return pl.pallas_call(
        _proj0_kernel,
        grid=(n // tm,),
        in_specs=[row(D_MODEL), _const_spec(w_in.shape), _const_spec(ln_g.shape), _const_spec(ln_b.shape),
                  _const_spec(w_s.shape), _const_spec(b_s.shape)],
        out_specs=[row(A_WIDTH)] * 4,
        out_shape=[out] * 4,
        compiler_params=pltpu.CompilerParams(dimension_semantics=("arbitrary",), vmem_limit_bytes=VMEM_LIMIT),
        name="proj0_sgu",
    )(x2d, w_in, ln_g, ln_b, w_s, b_s)


def _natten_kernel(q_ref, k_ref, v_ref, bias_ref, o_ref, *, rows, rb):
    i = pl.program_id(1)
    lane = lax.broadcasted_iota(jnp.int32, (GRID_W, LANES), 1)
    lo = lane < NA_HEAD_DIM
    nkeys = NA_KH * GRID_W
    ones = jnp.ones((nkeys, LANES), BF16)
    for rr in range(rb):
        r = i * rb + rr
        r_start = jnp.clip(r - NA_KH // 2, 0, rows - NA_KH)
        d = r - r_start
        kbase = pl.multiple_of(r_start * GRID_W, GRID_W)
        qrows = slice(rr * GRID_W, (rr + 1) * GRID_W)
        for hp in range(NA_HEADS // 2):
            cs = slice(hp * LANES, (hp + 1) * LANES)
            qp = q_ref[qrows, cs]
            z = jnp.zeros_like(qp)
            qq = jnp.concatenate([jnp.where(lo, qp, z), jnp.where(lo, z, qp)], axis=0)
            kk = k_ref[pl.ds(kbase, nkeys), cs]
            vv = jnp.concatenate([v_ref[pl.ds(kbase, nkeys), cs], ones], axis=1)
            s = _dot_nt(qq, kk) + bias_ref[d, hp]
            m = jnp.max(s, axis=-1, keepdims=True)
            pv = _dot(jnp.exp(s - m).astype(BF16), vv)
            pv = pv[:, :LANES] / pv[:, LANES:]
            o_ref[qrows, cs] = jnp.where(lo, pv[:GRID_W], pv[GRID_W:]).astype(BF16)


def _natten(q, k, v, bias, batch, seq):
    rows = seq // GRID_W
    rb = NA_ROWS_PER_STEP
    nblk = rows // rb
    kern = functools.partial(_natten_kernel, rows=rows, rb=rb)
    qspec = pl.BlockSpec((rb * GRID_W, NA_WIDTH), lambda b, i: (b * nblk + i, 0))
    kvspec = pl.BlockSpec((seq, NA_WIDTH), lambda b, i: (b, 0))
    return pl.pallas_call(
        kern,
        grid=(batch, nblk),
        in_specs=[qspec, kvspec, kvspec, _const_spec(bias.shape)],
        out_specs=qspec,
        out_shape=jax.ShapeDtypeStruct(q.shape, BF16),
        compiler_params=pltpu.CompilerParams(dimension_semantics=("arbitrary", "arbitrary"),
                                             vmem_limit_bytes=VMEM_LIMIT),
        name="natten",
    )(q, k, v, bias)


def _na_bias_table(rpb):
    c = np.arange(GRID_W)
    kc = np.arange(GRID_W)
    c_start = np.clip(c - NA_KW // 2, 0, GRID_W - NA_KW)
    valid = (kc[None, :] >= c_start[:, None]) & (kc[None, :] < c_start[:, None] + NA_KW)
    coff = kc[None, :] - c[:, None] + (NA_KW - 1)
    sel = (valid[None] & (coff[None] == np.arange(2 * NA_KW - 1)[:, None, None])).astype(np.float32)
    t = jnp.einsum("hro,ock->hcrk", rpb, jnp.asarray(sel), precision=lax.Precision.HIGHEST)
    t = t + jnp.asarray(np.where(valid, 0.0, NEG).astype(np.float32))[None, :, None, :]
    t = jnp.stack([t[:, :, NA_KH - 1 - d:2 * NA_KH - 1 - d] for d in range(NA_KH)])
    return t.reshape(NA_KH, NA_HEADS // 2, 2 * GRID_W, NA_KH * GRID_W).astype(F32)


def _post_kernel(*refs, n_in, sub):
    a_refs = refs[:n_in]
    wo_ref, x_ref, g1_ref, b1_ref, w1_ref, w2_ref, g2_ref, b2_ref, o_ref = refs[n_in:]
    subs = [slice(k * sub, (k + 1) * sub) for k in range(x_ref.shape[0] // sub)]
    x1s = []
    for rs in subs:
        a = jnp.concatenate([a_ref[rs, :] for a_ref in a_refs], axis=1)
        mix = _dot(a, wo_ref[...])
        x1s.append(_layer_norm(ALPHA * x_ref[rs, :] + mix, g1_ref[...], b1_ref[...]))
    for rs, x1 in zip(subs, x1s):
        x1b = x1.astype(BF16)
        acc = None
        for j in range(D_FF // FF_CHUNK):
            fs = slice(j * FF_CHUNK, (j + 1) * FF_CHUNK)
            hid = jnp.square(jnp.maximum(_dot(x1b, w1_ref[:, fs]), 0.0)).astype(BF16)
            part = _dot(hid, w2_ref[fs, :])
            acc = part if acc is None else acc + part
        o_ref[rs, :] = _layer_norm(ALPHA * x1 + acc, g2_ref[...], b2_ref[...])


def _post(acts, w_out, x2d, g1, b1, w1, w2, g2, b2):
    n = x2d.shape[0]
    tm = TM_POST
    n_in = len(acts)
    row = lambda width: pl.BlockSpec((tm, width), lambda i: (i, 0))
    in_specs = ([row(a.shape[1]) for a in acts] + [_const_spec(w_out.shape), row(D_MODEL)]
                + [_const_spec(p.shape) for p in (g1, b1, w1, w2, g2, b2)])
    return pl.pallas_call(
        functools.partial(_post_kernel, n_in=n_in, sub=SUB_POST),
        grid=(n // tm,),
        in_specs=in_specs,
        out_specs=row(D_MODEL),
        out_shape=jax.ShapeDtypeStruct((n, D_MODEL), F32),
        compiler_params=pltpu.CompilerParams(dimension_semantics=("arbitrary",), vmem_limit_bytes=VMEM_LIMIT),
        name="post",
    )(*acts, w_out, x2d, g1, b1, w1, w2, g2, b2)


def _proj1_kernel(x_ref, wqt_ref, wk_ref, wvt_ref, qt_ref, k_ref, vt_ref):
    xb = x_ref[...].astype(BF16)
    qt_ref[0] = (_dot_nt(wqt_ref[...], xb) * (DIFF_HEAD_DIM ** -0.5 * LOG2E)).astype(BF16)
    k_ref[...] = _dot(xb, wk_ref[...]).astype(BF16)
    vt_ref[0] = _dot_nt(wvt_ref[...], xb).astype(BF16)


def _proj1(x2d, wqt, wk, wvt):
    n = x2d.shape[0]
    tm = TK_DIFF
    row = pl.BlockSpec((tm, D_MODEL), lambda i: (i, 0))
    slab = pl.BlockSpec((1, D_MODEL, tm), lambda i: (i, 0, 0))
    slabs = jax.ShapeDtypeStruct((n // tm, D_MODEL, tm), BF16)
    return pl.pallas_call(
        _proj1_kernel,
        grid=(n // tm,),
        in_specs=[row, _const_spec(wqt.shape), _const_spec(wk.shape), _const_spec(wvt.shape)],
        out_specs=[slab, row, slab],
        out_shape=[slabs, jax.ShapeDtypeStruct((n, D_MODEL), BF16), slabs],
        compiler_params=pltpu.CompilerParams(dimension_semantics=("arbitrary",), vmem_limit_bytes=VMEM_LIMIT),
        name="proj1",
    )(x2d, wqt, wk, wvt)


def _diff_kernel(lam_ref, slope_ref, qt_ref, k_ref, vt_ref, feat_ref, g_ref, o_ref, c_ref, *, t, nk, nqb):
    h = pl.program_id(1)
    step = pl.program_id(2)
    slope = slope_ref[h] * LOG2E
    lam = lam_ref[0]
    hd = 2 * DIFF_HEAD_DIM
    ones_rows = 16

    cio = lax.broadcasted_iota(jnp.int32, (1, 2 * t), 1)
    qloc = jnp.where(cio >= t, cio - t, cio)

    @pl.when(step == 0)
    def _():
        kloc = lax.broadcasted_iota(jnp.int32, (t, 2 * t), 0)
        c_ref[...] = (-2.0 * slope) * jnp.maximum(kloc - qloc, 0).astype(F32)

    top = lax.broadcasted_iota(jnp.int32, (hd, t), 0) < DIFF_HEAD_DIM
    frows = 16
    frow = lax.broadcasted_iota(jnp.int32, (frows, 2 * t), 0)

    def split3(x):
        parts = []
        for _ in range(3):
            parts.append(x.astype(BF16).astype(F32))
            x = x - parts[-1]
        return parts

    qf = jnp.zeros((frows, 2 * t), F32)
    for term, part in enumerate(split3(jnp.full((1, 2 * t), slope, F32))):
        qf = jnp.where((frow >> 1) == term, part, qf)
    zpad = jnp.zeros((hd - frows, 2 * t), BF16)
    ones = jnp.ones((ones_rows, t), BF16)

    def block(qb):
        i = step * nqb + qb
        qt = qt_ref[qb]
        z = jnp.zeros_like(qt)
        qm = jnp.concatenate([jnp.where(top, qt, z), jnp.where(top, z, qt)], axis=1)

        def query_side(ref):
            rows = qf
            if ref is not None:
                for term, part in enumerate(split3(ref)):
                    rows = jnp.where(frow == 6 + term, -part, rows)
            return jnp.concatenate([qm, rows.astype(BF16), zpad], axis=0)

        def tile_index(jj):
            return jnp.where(jj == 0, i, (i + jj) & (nk - 1))

        def scores(jj, m=None):
            j = tile_index(jj)
            above = j > i
            cpos = slope * ((j - i) * t - qloc).astype(F32)
            cvec = jnp.where(above, -cpos, cpos)
            kt = k_ref[pl.ds(pl.multiple_of(j * t, t), t), :]
            ka = jnp.concatenate([kt, feat_ref[jnp.where(above, 1, 0)]], axis=1)
            s = _dot(ka, query_side(None if m is None else m - cvec))
            if isinstance(jj, int):
                s = s + c_ref[...] if jj == 0 else s
            else:
                s = s + jnp.where(jj == 0, 1.0, 0.0) * c_ref[...]
            return s, cvec

        def times_values(jj, p):
            vta = jnp.concatenate([vt_ref[tile_index(jj)], ones], axis=0)
            return _dot(vta, p.astype(BF16))

        def weighted_values(jj, s, ref):
            return times_values(jj, jnp.exp2(s - ref))

        def lagged_pass():
            s, cvec = scores(0)
            m = jnp.max(s, axis=0, keepdims=True) + cvec
            acc = weighted_values(0, s, m - cvec)
            excess = jnp.zeros_like(m)
            for jj in range(1, nk):
                s, _ = scores(jj, m)
                p = jnp.exp2(s).astype(BF16)
                r = times_values(jj, p)
                rise = jnp.maximum(jnp.log2(jnp.max(p, axis=0, keepdims=True).astype(F32)), 0.0)
                excess = jnp.maximum(excess, rise)
                acc = jnp.exp2(-rise) * (acc + r)
                m = m + rise
            return acc, excess

        def per_tile_max_pass():
            def tile(jj, carry):
                m, acc = carry
                s, cvec = scores(jj)
                m_new = jnp.maximum(m, jnp.max(s, axis=0, keepdims=True) + cvec)
                return m_new, jnp.exp2(m - m_new) * acc + weighted_values(jj, s, m_new - cvec)

            init = (jnp.full((1, 2 * t), -jnp.inf, F32), jnp.zeros((hd + ones_rows, 2 * t), F32))
            return lax.fori_loop(0, nk, tile, init)[1]

        return lagged_pass, per_tile_max_pass

    def finish(qb, acc):
        inv1 = 1.0 / acc[hd:hd + 1, :t]
        inv2 = 1.0 / acc[hd:hd + 1, t:]
        o = acc[:hd, :t] * inv1 - lam * (acc[:hd, t:] * inv2)
        ms = jnp.mean(o * o, axis=0, keepdims=True)
        o = o * lax.rsqrt(ms + LN_EPS) * g_ref[...] * (1.0 - LAM_INIT_1)
        o_ref[pl.ds(pl.multiple_of(qb * t, t), t), :] = o.T.astype(BF16)

    ok = None
    for qb in range(nqb):
        acc, excess = block(qb)[0]()
        finish(qb, acc)
        good = jnp.logical_and(jnp.sum(acc * 0.0) == 0.0, jnp.max(excess) <= LAG_MARGIN * LOG2E)
        ok = good if ok is None else jnp.logical_and(ok, good)

    @pl.when(jnp.logical_not(ok))
    def _():
        @pl.loop(0, nqb)
        def _(qb):
            finish(qb, block(qb)[1]())


def _diff_features(t):
    a = np.arange(t)
    lo = a % BF16_EXACT_INT
    f = np.zeros((t, 2 * DIFF_HEAD_DIM), np.float32)
    f[:, 0:6:2] = lo[:, None]
    f[:, 1:6:2] = (a - lo)[:, None]
    both = np.stack([f, -f])
    both[:, :, 6:9] = 1.0
    return jnp.asarray(both, BF16)


def _diffattn(lam, slopes, qt, k, vt, g_col, batch, seq):
    t = TK_DIFF
    nk = seq // t
    assert nk & (nk - 1) == 0
    hd = 2 * DIFF_HEAD_DIM
    feat = _diff_features(t)
    smem = pl.BlockSpec(memory_space=pltpu.SMEM)
    nqb = QBLOCKS_DIFF
    steps = nk // nqb
    return pl.pallas_call(
        functools.partial(_diff_kernel, t=t, nk=nk, nqb=nqb),
        grid=(batch, DIFF_HEADS, steps),
        in_specs=[smem, smem,
                  pl.BlockSpec((nqb, hd, t), lambda b, h, i: (b * steps + i, h, 0)),
                  pl.BlockSpec((seq, hd), lambda b, h, i: (b, h)),
                  pl.BlockSpec((nk, hd, t), lambda b, h, i: (b, h, 0)),
                  _const_spec(feat.shape),
                  _const_spec(g_col.shape)],
        out_specs=pl.BlockSpec((nqb * t, hd), lambda b, h, i: (b * steps + i, h)),
        out_shape=jax.ShapeDtypeStruct(k.shape, BF16),
        scratch_shapes=[pltpu.VMEM((t, 2 * t), F32)],
        compiler_params=pltpu.CompilerParams(dimension_semantics=("arbitrary",) * 3,
                                             vmem_limit_bytes=VMEM_LIMIT),
        name="diffattn",
    )(lam, slopes, qt, k, vt, feat, g_col)


def kernel(x_prompt, x_sample, l0_w_in, l0_w_out, l0_gate_ln_g, l0_gate_ln_b, l0_w_spatial, l0_b_spatial, l0_na_rpb, l0_ln1_g, l0_ln1_b, l0_w_ff1, l0_w_ff2, l0_ln2_g, l0_ln2_b, l1_w_in, l1_w_out, l1_lambda_q1, l1_lambda_k1, l1_lambda_q2, l1_lambda_k2, l1_subln_g, l1_ln1_g, l1_ln1_b, l1_w_ff1, l1_w_ff2, l1_ln2_g, l1_ln2_b):
    row = lambda p: p.reshape(1, -1).astype(F32)
    w_in0 = l0_w_in.astype(BF16)
    w_out0 = l0_w_out.astype(BF16)
    w_s = l0_w_spatial.astype(BF16)
    b_s = jnp.broadcast_to(l0_b_spatial.astype(F32)[:, :, None], (A_GROUPS, CHUNK, LANES))
    na_bias = _na_bias_table(l0_na_rpb.astype(F32))
    wqt1 = l1_w_in[:, :D_MODEL].T.astype(BF16)
    wk1 = l1_w_in[:, D_MODEL:2 * D_MODEL].astype(BF16)
    wvt1 = l1_w_in[:, 2 * D_MODEL:].T.astype(BF16)
    w_out1 = l1_w_out.astype(BF16)
    lam1 = jnp.exp(jnp.sum((l1_lambda_q1 * l1_lambda_k1).astype(F32)))
    lam2 = jnp.exp(jnp.sum((l1_lambda_q2 * l1_lambda_k2).astype(F32)))
    lam = (lam1 - lam2 + LAM_INIT_1).reshape(1).astype(F32)
    slopes = jnp.exp2(-8.0 * jnp.arange(1, DIFF_HEADS + 1, dtype=F32) / DIFF_HEADS)
    g_col = l1_subln_g.astype(F32).reshape(-1, 1)
    ff0 = (l0_w_ff1.astype(BF16), l0_w_ff2.astype(BF16))
    ff1 = (l1_w_ff1.astype(BF16), l1_w_ff2.astype(BF16))

    def trunk(x):
        batch, seq, _ = x.shape
        x2d = x.reshape(batch * seq, D_MODEL)
        out_a, q, k, v = _proj0(x2d, w_in0, row(l0_gate_ln_g), row(l0_gate_ln_b), w_s, b_s)
        out_b = _natten(q, k, v, na_bias, batch, seq)
        x2d = _post([out_a, out_b], w_out0, x2d, row(l0_ln1_g), row(l0_ln1_b),
                    ff0[0], ff0[1], row(l0_ln2_g), row(l0_ln2_b))
        qt, k, vt = _proj1(x2d, wqt1, wk1, wvt1)
        o = _diffattn(lam, slopes, qt, k, vt, g_col, batch, seq)
        x2d = _post([o], w_out1, x2d, row(l1_ln1_g), row(l1_ln1_b),
                    ff1[0], ff1[1], row(l1_ln2_g), row(l1_ln2_b))
        return x2d.reshape(batch, seq, D_MODEL)

    return (trunk(x_prompt), trunk(x_sample))
```

```python
import functools
import math

import jax
import jax.numpy as jnp
import numpy as np
from jax import lax
from jax.experimental import pallas as pl
from jax.experimental.pallas import tpu as pltpu

F32 = jnp.float32
BF16 = jnp.bfloat16

D_MODEL = 1024
DEPTH = 2
GRID_W = 64
CHUNK = 128
A_WIDTH = 512
A_GROUPS = 4
NA_HEADS = 8
NA_HEAD_DIM = 64
NA_WIDTH = 512
NA_KH = 8
NA_KW = 16
DIFF_HEADS = 8
DIFF_HEAD_DIM = 64
D_FF = 4 * D_MODEL
LN_EPS = 1e-5
ALPHA = (2 * DEPTH) ** 0.25
LAM_INIT_1 = 0.8 - 0.6 * math.exp(-0.3 * 1)
NEG = -1e30
LAG_MARGIN = 60.0
LOG2E = float(np.float32(1.0 / math.log(2.0)))

LANES = 128
BF16_EXACT_INT = 256
VMEM_LIMIT = 56 * 1024 * 1024

TM_PROJ0 = 1024
TM_POST = 1024
SUB_POST = 256
TK_DIFF = 512
QBLOCKS_DIFF = 4
NA_ROWS_PER_STEP = 16
FF_CHUNK = 1024


def _const_spec(shape):
    nd = len(shape)
    return pl.BlockSpec(shape, lambda *_: (0,) * nd, pipeline_mode=pl.Buffered(1))


def _layer_norm(y, g, b):
    mu = jnp.mean(y, axis=-1, keepdims=True)
    yc = y - mu
    var = jnp.mean(yc * yc, axis=-1, keepdims=True)
    return yc * lax.rsqrt(var + LN_EPS) * g + b


def _dot(a, b):
    return jnp.dot(a, b, preferred_element_type=F32)


def _dot_nt(a, b):
    return lax.dot_general(a, b, (((1,), (1,)), ((), ())), preferred_element_type=F32)


def _proj0_kernel(x_ref, w_ref, lng_ref, lnb_ref, ws_ref, bs_ref, oa_ref, q_ref, k_ref, v_ref):
    tm = x_ref.shape[0]
    xb = x_ref[...].astype(BF16)

    def mm(lo, hi):
        return _dot(xb, w_ref[:, lo:hi])

    u = jax.nn.gelu(mm(0, A_WIDTH))
    v = jax.nn.gelu(mm(A_WIDTH, 2 * A_WIDTH))
    vn = _layer_norm(v, lng_ref[...], lnb_ref[...]).astype(BF16)
    for c in range(tm // CHUNK):
        rs = slice(c * CHUNK, (c + 1) * CHUNK)
        for g in range(A_GROUPS):
            cs = slice(g * LANES, (g + 1) * LANES)
            mixed = _dot(ws_ref[g], vn[rs, cs]) + bs_ref[g]
            oa_ref[rs, cs] = (u[rs, cs] * mixed).astype(BF16)
    base = 2 * A_WIDTH
    q_ref[...] = (mm(base, base + NA_WIDTH) * (NA_HEAD_DIM ** -0.5)).astype(BF16)
    k_ref[...] = mm(base + NA_WIDTH, base + 2 * NA_WIDTH).astype(BF16)
    v_ref[...] = mm(base + 2 * NA_WIDTH, base + 3 * NA_WIDTH).astype(BF16)


def _proj0(x2d, w_in, ln_g, ln_b, w_s, b_s):
    n = x2d.shape[0]
    tm = TM_PROJ0
    row = lambda width: pl.BlockSpec((tm, width), lambda i: (i, 0))
    out = jax.ShapeDtypeStruct((n, A_WIDTH), BF16)
    return pl.pallas_call(
        _proj0_kernel,
        grid=(n // tm,),
        in_specs=[row(D_MODEL), _const_spec(w_in.shape), _const_spec(ln_g.shape), _const_spec(ln_b.shape),
                  _const_spec(w_s.shape), _const_spec(b_s.shape)],
        out_specs=[row(A_WIDTH)] * 4,
        out_shape=[out] * 4,
        compiler_params=pltpu.CompilerParams(dimension_semantics=("arbitrary",), vmem_limit_bytes=VMEM_LIMIT),
        name="proj0_sgu",
    )(x2d, w_in, ln_g, ln_b, w_s, b_s)


def _natten_kernel(q_ref, k_ref, v_ref, bias_ref, o_ref, *, rows, rb):
    i = pl.program_id(1)
    lane = lax.broadcasted_iota(jnp.int32, (GRID_W, LANES), 1)
    lo = lane < NA_HEAD_DIM
    nkeys = NA_KH * GRID_W
    ones = jnp.ones((nkeys, LANES), BF16)
    for rr in range(rb):
        r = i * rb + rr
        r_start = jnp.clip(r - NA_KH // 2, 0, rows - NA_KH)
        d = r - r_start
        kbase = pl.multiple_of(r_start * GRID_W, GRID_W)
        qrows = slice(rr * GRID_W, (rr + 1) * GRID_W)
        for hp in range(NA_HEADS // 2):
            cs = slice(hp * LANES, (hp + 1) * LANES)
            qp = q_ref[qrows, cs]
            z = jnp.zeros_like(qp)
            qq = jnp.concatenate([jnp.where(lo, qp, z), jnp.where(lo, z, qp)], axis=0)
            kk = k_ref[pl.ds(kbase, nkeys), cs]
            vv = jnp.concatenate([v_ref[pl.ds(kbase, nkeys), cs], ones], axis=1)
            s = _dot_nt(qq, kk) + bias_ref[d, hp]
            m = jnp.max(s, axis=-1, keepdims=True)
            pv = _dot(jnp.exp(s - m).astype(BF16), vv)
            pv = pv[:, :LANES] / pv[:, LANES:]
            o_ref[qrows, cs] = jnp.where(lo, pv[:GRID_W], pv[GRID_W:]).astype(BF16)


def _natten(q, k, v, bias, batch, seq):
    rows = seq // GRID_W
    rb = NA_ROWS_PER_STEP
    nblk = rows // rb
    kern = functools.partial(_natten_kernel, rows=rows, rb=rb)
    qspec = pl.BlockSpec((rb * GRID_W, NA_WIDTH), lambda b, i: (b * nblk + i, 0))
    kvspec = pl.BlockSpec((seq, NA_WIDTH), lambda b, i: (b, 0))
    return pl.pallas_call(
        kern,
        grid=(batch, nblk),
        in_specs=[qspec, kvspec, kvspec, _const_spec(bias.shape)],
        out_specs=qspec,
        out_shape=jax.ShapeDtypeStruct(q.shape, BF16),
        compiler_params=pltpu.CompilerParams(dimension_semantics=("arbitrary", "arbitrary"),
                                             vmem_limit_bytes=VMEM_LIMIT),
        name="natten",
    )(q, k, v, bias)


def _na_bias_table(rpb):
    c = np.arange(GRID_W)
    kc = np.arange(GRID_W)
    c_start = np.clip(c - NA_KW // 2, 0, GRID_W - NA_KW)
    valid = (kc[None, :] >= c_start[:, None]) & (kc[None, :] < c_start[:, None] + NA_KW)
    coff = kc[None, :] - c[:, None] + (NA_KW - 1)
    sel = (valid[None] & (coff[None] == np.arange(2 * NA_KW - 1)[:, None, None])).astype(np.float32)
    t = jnp.einsum("hro,ock->hcrk", rpb, jnp.asarray(sel), precision=lax.Precision.HIGHEST)
    t = t + jnp.asarray(np.where(valid, 0.0, NEG).astype(np.float32))[None, :, None, :]
    t = jnp.stack([t[:, :, NA_KH - 1 - d:2 * NA_KH - 1 - d] for d in range(NA_KH)])
    return t.reshape(NA_KH, NA_HEADS // 2, 2 * GRID_W, NA_KH * GRID_W).astype(F32)


def _post_kernel(*refs, n_in, sub):
    a_refs = refs[:n_in]
    wo_ref, x_ref, g1_ref, b1_ref, w1_ref, w2_ref, g2_ref, b2_ref, o_ref = refs[n_in:]
    subs = [slice(k * sub, (k + 1) * sub) for k in range(x_ref.shape[0] // sub)]
    x1s = []
    for rs in subs:
        a = jnp.concatenate([a_ref[rs, :] for a_ref in a_refs], axis=1)
        mix = _dot(a, wo_ref[...])
        x1s.append(_layer_norm(ALPHA * x_ref[rs, :] + mix, g1_ref[...], b1_ref[...]))
    for rs, x1 in zip(subs, x1s):
        x1b = x1.astype(BF16)
        acc = None
        for j in range(D_FF // FF_CHUNK):
            fs = slice(j * FF_CHUNK, (j + 1) * FF_CHUNK)
            hid = jnp.square(jnp.maximum(_dot(x1b, w1_ref[:, fs]), 0.0)).astype(BF16)
            part = _dot(hid, w2_ref[fs, :])
            acc = part if acc is None else acc + part
        o_ref[rs, :] = _layer_norm(ALPHA * x1 + acc, g2_ref[...], b2_ref[...])


def _post(acts, w_out, x2d, g1, b1, w1, w2, g2, b2):
    n = x2d.shape[0]
    tm = TM_POST
    n_in = len(acts)
    row = lambda width: pl.BlockSpec((tm, width), lambda i: (i, 0))
    in_specs = ([row(a.shape[1]) for a in acts] + [_const_spec(w_out.shape), row(D_MODEL)]
                + [_const_spec(p.shape) for p in (g1, b1, w1, w2, g2, b2)])
    return pl.pallas_call(
        functools.partial(_post_kernel, n_in=n_in, sub=SUB_POST),
        grid=(n // tm,),
        in_specs=in_specs,
        out_specs=row(D_MODEL),
        out_shape=jax.ShapeDtypeStruct((n, D_MODEL), F32),
        compiler_params=pltpu.CompilerParams(dimension_semantics=("arbitrary",), vmem_limit_bytes=VMEM_LIMIT),
        name="post",
    )(*acts, w_out, x2d, g1, b1, w1, w2, g2, b2)


def _proj1_kernel(x_ref, wqt_ref, wk_ref, wvt_ref, qt_ref, k_ref, vt_ref):
    xb = x_ref[...].astype(BF16)
    qt_ref[0] = (_dot_nt(wqt_ref[...], xb) * (DIFF_HEAD_DIM ** -0.5 * LOG2E)).astype(BF16)
    k_ref[...] = _dot(xb, wk_ref[...]).astype(BF16)
    vt_ref[0] = _dot_nt(wvt_ref[...], xb).astype(BF16)


def _proj1(x2d, wqt, wk, wvt):
    n = x2d.shape[0]
    tm = TK_DIFF
    row = pl.BlockSpec((tm, D_MODEL), lambda i: (i, 0))
    slab = pl.BlockSpec((1, D_MODEL, tm), lambda i: (i, 0, 0))
    slabs = jax.ShapeDtypeStruct((n // tm, D_MODEL, tm), BF16)
    return pl.pallas_call(
        _proj1_kernel,
        grid=(n // tm,),
        in_specs=[row, _const_spec(wqt.shape), _const_spec(wk.shape), _const_spec(wvt.shape)],
        out_specs=[slab, row, slab],
        out_shape=[slabs, jax.ShapeDtypeStruct((n, D_MODEL), BF16), slabs],
        compiler_params=pltpu.CompilerParams(dimension_semantics=("arbitrary",), vmem_limit_bytes=VMEM_LIMIT),
        name="proj1",
    )(x2d, wqt, wk, wvt)


def _diff_kernel(lam_ref, slope_ref, qt_ref, k_ref, vt_ref, feat_ref, g_ref, o_ref, c_ref, *, t, nk, nqb):
    h = pl.program_id(1)
    step = pl.program_id(2)
    slope = slope_ref[h] * LOG2E
    lam = lam_ref[0]
    hd = 2 * DIFF_HEAD_DIM
    ones_rows = 16

    cio = lax.broadcasted_iota(jnp.int32, (1, 2 * t), 1)
    qloc = jnp.where(cio >= t, cio - t, cio)

    @pl.when(step == 0)
    def _():
        kloc = lax.broadcasted_iota(jnp.int32, (t, 2 * t), 0)
        c_ref[...] = (-2.0 * slope) * jnp.maximum(kloc - qloc, 0).astype(F32)

    top = lax.broadcasted_iota(jnp.int32, (hd, t), 0) < DIFF_HEAD_DIM
    frow = lax.broadcasted_iota(jnp.int32, (hd, 2 * t), 0)
    rest = jnp.full((hd, 2 * t), slope, F32)
    qf = jnp.zeros((hd, 2 * t), F32)
    for term in range(3):
        part = rest.astype(BF16).astype(F32)
        qf = jnp.where((frow >> 1) == term, part, qf)
        rest = rest - part
    qf = qf.astype(BF16)
    ones = jnp.ones((ones_rows, t), BF16)

    def block(qb):
        i = step * nqb + qb
        qt = qt_ref[qb]
        z = jnp.zeros_like(qt)
        qm = jnp.concatenate([jnp.where(top, qt, z), jnp.where(top, z, qt)], axis=1)
        qa = jnp.concatenate([qm, qf], axis=0)

        def tile_index(jj):
            return jnp.where(jj == 0, i, (i + jj) & (nk - 1))

        def scores(jj):
            j = tile_index(jj)
            above = j > i
            kt = k_ref[pl.ds(pl.multiple_of(j * t, t), t), :]
            ka = jnp.concatenate([kt, feat_ref[jnp.where(above, 1, 0)]], axis=1)
            s = _dot(ka, qa)
            if isinstance(jj, int):
                s = s + c_ref[...] if jj == 0 else s
            else:
                s = s + jnp.where(jj == 0, 1.0, 0.0) * c_ref[...]
            cpos = slope * ((j - i) * t - qloc).astype(F32)
            cvec = jnp.where(above, -cpos, cpos)
            return s, cvec

        def weighted_values(jj, s, ref):
            vta = jnp.concatenate([vt_ref[tile_index(jj)], ones], axis=0)
            return _dot(vta, jnp.exp2(s - ref).astype(BF16))

        def lagged_pass():
            s, cvec = scores(0)
            m = jnp.max(s, axis=0, keepdims=True) + cvec
            acc = weighted_values(0, s, m - cvec)
            excess = jnp.zeros_like(m)
            for jj in range(1, nk):
                s, cvec = scores(jj)
                r = weighted_values(jj, s, m - cvec)
                smax = jnp.max(s, axis=0, keepdims=True) + cvec
                m_new = jnp.maximum(m, smax)
                excess = jnp.maximum(excess, smax - m)
                acc = jnp.exp2(m - m_new) * (acc + r)
                m = m_new
            return acc, excess

        def per_tile_max_pass():
            def tile(jj, carry):
                m, acc = carry
                s, cvec = scores(jj)
                m_new = jnp.maximum(m, jnp.max(s, axis=0, keepdims=True) + cvec)
                return m_new, jnp.exp2(m - m_new) * acc + weighted_values(jj, s, m_new - cvec)

            init = (jnp.full((1, 2 * t), -jnp.inf, F32), jnp.zeros((hd + ones_rows, 2 * t), F32))
            return lax.fori_loop(0, nk, tile, init)[1]

        return lagged_pass, per_tile_max_pass

    def finish(qb, acc):
        inv1 = 1.0 / acc[hd:hd + 1, :t]
        inv2 = 1.0 / acc[hd:hd + 1, t:]
        o = acc[:hd, :t] * inv1 - lam * (acc[:hd, t:] * inv2)
        ms = jnp.mean(o * o, axis=0, keepdims=True)
        o = o * lax.rsqrt(ms + LN_EPS) * g_ref[...] * (1.0 - LAM_INIT_1)
        o_ref[pl.ds(pl.multiple_of(qb * t, t), t), :] = o.T.astype(BF16)

    ok = None
    for qb in range(nqb):
        acc, excess = block(qb)[0]()
        finish(qb, acc)
        good = jnp.logical_and(jnp.sum(acc * 0.0) == 0.0, jnp.max(excess) <= LAG_MARGIN * LOG2E)
        ok = good if ok is None else jnp.logical_and(ok, good)

    @pl.when(jnp.logical_not(ok))
    def _():
        @pl.loop(0, nqb)
        def _(qb):
            finish(qb, block(qb)[1]())


def _diff_features(t):
    a = np.arange(t)
    lo = a % BF16_EXACT_INT
    f = np.zeros((t, 2 * DIFF_HEAD_DIM), np.float32)
    f[:, 0:6:2] = lo[:, None]
    f[:, 1:6:2] = (a - lo)[:, None]
    return jnp.asarray(np.stack([f, -f]), BF16)


def _diffattn(lam, slopes, qt, k, vt, g_col, batch, seq):
    t = TK_DIFF
    nk = seq // t
    assert nk & (nk - 1) == 0
    hd = 2 * DIFF_HEAD_DIM
    feat = _diff_features(t)
    smem = pl.BlockSpec(memory_space=pltpu.SMEM)
    nqb = QBLOCKS_DIFF
    steps = nk // nqb
    return pl.pallas_call(
        functools.partial(_diff_kernel, t=t, nk=nk, nqb=nqb),
        grid=(batch, DIFF_HEADS, steps),
        in_specs=[smem, smem,
                  pl.BlockSpec((nqb, hd, t), lambda b, h, i: (b * steps + i, h, 0)),
                  pl.BlockSpec((seq, hd), lambda b, h, i: (b, h)),
                  pl.BlockSpec((nk, hd, t), lambda b, h, i: (b, h, 0)),
                  _const_spec(feat.shape),
                  _const_spec(g_col.shape)],
        out_specs=pl.BlockSpec((nqb * t, hd), lambda b, h, i: (b * steps + i, h)),
        out_shape=jax.ShapeDtypeStruct(k.shape, BF16),
        scratch_shapes=[pltpu.VMEM((t, 2 * t), F32)],
        compiler_params=pltpu.CompilerParams(dimension_semantics=("arbitrary",) * 3,
                                             vmem_limit_bytes=VMEM_LIMIT),
        name="diffattn",
    )(lam, slopes, qt, k, vt, feat, g_col)


def kernel(x_prompt, x_sample, l0_w_in, l0_w_out, l0_gate_ln_g, l0_gate_ln_b, l0_w_spatial, l0_b_spatial, l0_na_rpb, l0_ln1_g, l0_ln1_b, l0_w_ff1, l0_w_ff2, l0_ln2_g, l0_ln2_b, l1_w_in, l1_w_out, l1_lambda_q1, l1_lambda_k1, l1_lambda_q2, l1_lambda_k2, l1_subln_g, l1_ln1_g, l1_ln1_b, l1_w_ff1, l1_w_ff2, l1_ln2_g, l1_ln2_b):
    row = lambda p: p.reshape(1, -1).astype(F32)
    w_in0 = l0_w_in.astype(BF16)
    w_out0 = l0_w_out.astype(BF16)
    w_s = l0_w_spatial.astype(BF16)
    b_s = jnp.broadcast_to(l0_b_spatial.astype(F32)[:, :, None], (A_GROUPS, CHUNK, LANES))
    na_bias = _na_bias_table(l0_na_rpb.astype(F32))
    wqt1 = l1_w_in[:, :D_MODEL].T.astype(BF16)
    wk1 = l1_w_in[:, D_MODEL:2 * D_MODEL].astype(BF16)
    wvt1 = l1_w_in[:, 2 * D_MODEL:].T.astype(BF16)
    w_out1 = l1_w_out.astype(BF16)
    lam1 = jnp.exp(jnp.sum((l1_lambda_q1 * l1_lambda_k1).astype(F32)))
    lam2 = jnp.exp(jnp.sum((l1_lambda_q2 * l1_lambda_k2).astype(F32)))
    lam = (lam1 - lam2 + LAM_INIT_1).reshape(1).astype(F32)
    slopes = jnp.exp2(-8.0 * jnp.arange(1, DIFF_HEADS + 1, dtype=F32) / DIFF_HEADS)
    g_col = l1_subln_g.astype(F32).reshape(-1, 1)
    ff0 = (l0_w_ff1.astype(BF16), l0_w_ff2.astype(BF16))
    ff1 = (l1_w_ff1.astype(BF16), l1_w_ff2.astype(BF16))

    def trunk(x):
        batch, seq, _ = x.shape
        x2d = x.reshape(batch * seq, D_MODEL)
        out_a, q, k, v = _proj0(x2d, w_in0, row(l0_gate_ln_g), row(l0_gate_ln_b), w_s, b_s)
        out_b = _natten(q, k, v, na_bias, batch, seq)
        x2d = _post([out_a, out_b], w_out0, x2d, row(l0_ln1_g), row(l0_ln1_b),
                    ff0[0], ff0[1], row(l0_ln2_g), row(l0_ln2_b))
        qt, k, vt = _proj1(x2d, wqt1, wk1, wvt1)
        o = _diffattn(lam, slopes, qt, k, vt, g_col, batch, seq)
        x2d = _post([o], w_out1, x2d, row(l1_ln1_g), row(l1_ln1_b),
                    ff1[0], ff1[1], row(l1_ln2_g), row(l1_ln2_b))
        return x2d.reshape(batch, seq, D_MODEL)

    return (trunk(x_prompt), trunk(x_sample))
```

```python
import functools
import math

import jax
import jax.numpy as jnp
import numpy as np
from jax import lax
from jax.experimental import pallas as pl
from jax.experimental.pallas import tpu as pltpu

F32 = jnp.float32
BF16 = jnp.bfloat16

D_MODEL = 1024
DEPTH = 2
GRID_W = 64
CHUNK = 128
A_WIDTH = 512
A_GROUPS = 4
NA_HEADS = 8
NA_HEAD_DIM = 64
NA_WIDTH = 512
NA_KH = 8
NA_KW = 16
DIFF_HEADS = 8
DIFF_HEAD_DIM = 64
D_FF = 4 * D_MODEL
LN_EPS = 1e-5
ALPHA = (2 * DEPTH) ** 0.25
LAM_INIT_1 = 0.8 - 0.6 * math.exp(-0.3 * 1)
NEG = -1e30
LAG_MARGIN = 60.0
LOG2E = float(np.float32(1.0 / math.log(2.0)))

LANES = 128
BF16_EXACT_INT = 256
VMEM_LIMIT = 56 * 1024 * 1024

TM_PROJ0 = 1024
TM_POST = 1024
SUB_POST = 256
TK_DIFF = 512
QBLOCKS_DIFF = 4
NA_ROWS_PER_STEP = 16
FF_CHUNK = 1024


def _const_spec(shape):
    nd = len(shape)
    return pl.BlockSpec(shape, lambda *_: (0,) * nd, pipeline_mode=pl.Buffered(1))


def _layer_norm(y, g, b):
    mu = jnp.mean(y, axis=-1, keepdims=True)
    yc = y - mu
    var = jnp.mean(yc * yc, axis=-1, keepdims=True)
    return yc * lax.rsqrt(var + LN_EPS) * g + b


def _dot(a, b):
    return jnp.dot(a, b, preferred_element_type=F32)


def _dot_nt(a, b):
    return lax.dot_general(a, b, (((1,), (1,)), ((), ())), preferred_element_type=F32)


def _proj0_kernel(x_ref, w_ref, lng_ref, lnb_ref, ws_ref, bs_ref, oa_ref, q_ref, k_ref, v_ref):
    tm = x_ref.shape[0]
    xb = x_ref[...].astype(BF16)

    def mm(lo, hi):
        return _dot(xb, w_ref[:, lo:hi])

    u = jax.nn.gelu(mm(0, A_WIDTH))
    v = jax.nn.gelu(mm(A_WIDTH, 2 * A_WIDTH))
    vn = _layer_norm(v, lng_ref[...], lnb_ref[...]).astype(BF16)
    for c in range(tm // CHUNK):
        rs = slice(c * CHUNK, (c + 1) * CHUNK)
        for g in range(A_GROUPS):
            cs = slice(g * LANES, (g + 1) * LANES)
            mixed = _dot(ws_ref[g], vn[rs, cs]) + bs_ref[g]
            oa_ref[rs, cs] = (u[rs, cs] * mixed).astype(BF16)
    base = 2 * A_WIDTH
    q_ref[...] = (mm(base, base + NA_WIDTH) * (NA_HEAD_DIM ** -0.5)).astype(BF16)
    k_ref[...] = mm(base + NA_WIDTH, base + 2 * NA_WIDTH).astype(BF16)
    v_ref[...] = mm(base + 2 * NA_WIDTH, base + 3 * NA_WIDTH).astype(BF16)


def _proj0(x2d, w_in, ln_g, ln_b, w_s, b_s):
    n = x2d.shape[0]
    tm = TM_PROJ0
    row = lambda width: pl.BlockSpec((tm, width), lambda i: (i, 0))
    out = jax.ShapeDtypeStruct((n, A_WIDTH), BF16)
    return pl.pallas_call(
        _proj0_kernel,
        grid=(n // tm,),
        in_specs=[row(D_MODEL), _const_spec(w_in.shape), _const_spec(ln_g.shape), _const_spec(ln_b.shape),
                  _const_spec(w_s.shape), _const_spec(b_s.shape)],
        out_specs=[row(A_WIDTH)] * 4,
        out_shape=[out] * 4,
        compiler_params=pltpu.CompilerParams(dimension_semantics=("arbitrary",), vmem_limit_bytes=VMEM_LIMIT),
        name="proj0_sgu",
    )(x2d, w_in, ln_g, ln_b, w_s, b_s)


def _natten_kernel(q_ref, k_ref, v_ref, bias_ref, o_ref, *, rows, rb):
    i = pl.program_id(1)
    lane = lax.broadcasted_iota(jnp.int32, (GRID_W, LANES), 1)
    lo = lane < NA_HEAD_DIM
    nkeys = NA_KH * GRID_W
    ones = jnp.ones((nkeys, LANES), BF16)
    for rr in range(rb):
        r = i * rb + rr
        r_start = jnp.clip(r - NA_KH // 2, 0, rows - NA_KH)
        d = r - r_start
        kbase = pl.multiple_of(r_start * GRID_W, GRID_W)
        qrows = slice(rr * GRID_W, (rr + 1) * GRID_W)
        for hp in range(NA_HEADS // 2):
            cs = slice(hp * LANES, (hp + 1) * LANES)
            qp = q_ref[qrows, cs]
            z = jnp.zeros_like(qp)
            qq = jnp.concatenate([jnp.where(lo, qp, z), jnp.where(lo, z, qp)], axis=0)
            kk = k_ref[pl.ds(kbase, nkeys), cs]
            vv = jnp.concatenate([v_ref[pl.ds(kbase, nkeys), cs], ones], axis=1)
            s = _dot_nt(qq, kk) + bias_ref[d, hp]
            m = jnp.max(s, axis=-1, keepdims=True)
            pv = _dot(jnp.exp(s - m).astype(BF16), vv)
            pv = pv[:, :LANES] / pv[:, LANES:]
            o_ref[qrows, cs] = jnp.where(lo, pv[:GRID_W], pv[GRID_W:]).astype(BF16)


def _natten(q, k, v, bias, batch, seq):
    rows = seq // GRID_W
    rb = NA_ROWS_PER_STEP
    nblk = rows // rb
    kern = functools.partial(_natten_kernel, rows=rows, rb=rb)
    qspec = pl.BlockSpec((rb * GRID_W, NA_WIDTH), lambda b, i: (b * nblk + i, 0))
    kvspec = pl.BlockSpec((seq, NA_WIDTH), lambda b, i: (b, 0))
    return pl.pallas_call(
        kern,
        grid=(batch, nblk),
        in_specs=[qspec, kvspec, kvspec, _const_spec(bias.shape)],
        out_specs=qspec,
        out_shape=jax.ShapeDtypeStruct(q.shape, BF16),
        compiler_params=pltpu.CompilerParams(dimension_semantics=("arbitrary", "arbitrary"),
                                             vmem_limit_bytes=VMEM_LIMIT),
        name="natten",
    )(q, k, v, bias)


def _na_bias_table(rpb):
    c = np.arange(GRID_W)
    kc = np.arange(GRID_W)
    c_start = np.clip(c - NA_KW // 2, 0, GRID_W - NA_KW)
    valid = (kc[None, :] >= c_start[:, None]) & (kc[None, :] < c_start[:, None] + NA_KW)
    coff = kc[None, :] - c[:, None] + (NA_KW - 1)
    sel = (valid[None] & (coff[None] == np.arange(2 * NA_KW - 1)[:, None, None])).astype(np.float32)
    t = jnp.einsum("hro,ock->hcrk", rpb, jnp.asarray(sel), precision=lax.Precision.HIGHEST)
    t = t + jnp.asarray(np.where(valid, 0.0, NEG).astype(np.float32))[None, :, None, :]
    t = jnp.stack([t[:, :, NA_KH - 1 - d:2 * NA_KH - 1 - d] for d in range(NA_KH)])
    return t.reshape(NA_KH, NA_HEADS // 2, 2 * GRID_W, NA_KH * GRID_W).astype(F32)


def _post_kernel(*refs, n_in, sub):
    a_refs = refs[:n_in]
    wo_ref, x_ref, g1_ref, b1_ref, w1_ref, w2_ref, g2_ref, b2_ref, o_ref = refs[n_in:]
    subs = [slice(k * sub, (k + 1) * sub) for k in range(x_ref.shape[0] // sub)]
    x1s = []
    for rs in subs:
        a = jnp.concatenate([a_ref[rs, :] for a_ref in a_refs], axis=1)
        mix = _dot(a, wo_ref[...])
        x1s.append(_layer_norm(ALPHA * x_ref[rs, :] + mix, g1_ref[...], b1_ref[...]))
    for rs, x1 in zip(subs, x1s):
        x1b = x1.astype(BF16)
        acc = None
        for j in range(D_FF // FF_CHUNK):
            fs = slice(j * FF_CHUNK, (j + 1) * FF_CHUNK)
            hid = jnp.square(jnp.maximum(_dot(x1b, w1_ref[:, fs]), 0.0)).astype(BF16)
            part = _dot(hid, w2_ref[fs, :])
            acc = part if acc is None else acc + part
        o_ref[rs, :] = _layer_norm(ALPHA * x1 + acc, g2_ref[...], b2_ref[...])


def _post(acts, w_out, x2d, g1, b1, w1, w2, g2, b2):
    n = x2d.shape[0]
    tm = TM_POST
    n_in = len(acts)
    row = lambda width: pl.BlockSpec((tm, width), lambda i: (i, 0))
    in_specs = ([row(a.shape[1]) for a in acts] + [_const_spec(w_out.shape), row(D_MODEL)]
                + [_const_spec(p.shape) for p in (g1, b1, w1, w2, g2, b2)])
    return pl.pallas_call(
        functools.partial(_post_kernel, n_in=n_in, sub=SUB_POST),
        grid=(n // tm,),
        in_specs=in_specs,
        out_specs=row(D_MODEL),
        out_shape=jax.ShapeDtypeStruct((n, D_MODEL), F32),
        compiler_params=pltpu.CompilerParams(dimension_semantics=("arbitrary",), vmem_limit_bytes=VMEM_LIMIT),
        name="post",
    )(*acts, w_out, x2d, g1, b1, w1, w2, g2, b2)


def _proj1_kernel(x_ref, wqt_ref, wk_ref, wvt_ref, qt_ref, k_ref, vt_ref):
    xb = x_ref[...].astype(BF16)
    qt_ref[0] = (_dot_nt(wqt_ref[...], xb) * (DIFF_HEAD_DIM ** -0.5 * LOG2E)).astype(BF16)
    k_ref[...] = _dot(xb, wk_ref[...]).astype(BF16)
    vt_ref[0] = _dot_nt(wvt_ref[...], xb).astype(BF16)


def _proj1(x2d, wqt, wk, wvt):
    n = x2d.shape[0]
    tm = TK_DIFF
    row = pl.BlockSpec((tm, D_MODEL), lambda i: (i, 0))
    slab = pl.BlockSpec((1, D_MODEL, tm), lambda i: (i, 0, 0))
    slabs = jax.ShapeDtypeStruct((n // tm, D_MODEL, tm), BF16)
    return pl.pallas_call(
        _proj1_kernel,
        grid=(n // tm,),
        in_specs=[row, _const_spec(wqt.shape), _const_spec(wk.shape), _const_spec(wvt.shape)],
        out_specs=[slab, row, slab],
        out_shape=[slabs, jax.ShapeDtypeStruct((n, D_MODEL), BF16), slabs],
        compiler_params=pltpu.CompilerParams(dimension_semantics=("arbitrary",), vmem_limit_bytes=VMEM_LIMIT),
        name="proj1",
    )(x2d, wqt, wk, wvt)


def _diff_kernel(lam_ref, slope_ref, qt_ref, k_ref, vt_ref, feat_ref, g_ref, o_ref, c_ref, *, t, nk, nqb):
    h = pl.program_id(1)
    step = pl.program_id(2)
    slope = slope_ref[h] * LOG2E
    lam = lam_ref[0]
    hd = 2 * DIFF_HEAD_DIM
    ones_rows = 16

    cio = lax.broadcasted_iota(jnp.int32, (1, 2 * t), 1)
    qloc = jnp.where(cio >= t, cio - t, cio)

    @pl.when(step == 0)
    def _():
        kloc = lax.broadcasted_iota(jnp.int32, (t, 2 * t), 0)
        c_ref[...] = (-2.0 * slope) * jnp.maximum(kloc - qloc, 0).astype(F32)

    top = lax.broadcasted_iota(jnp.int32, (hd, t), 0) < DIFF_HEAD_DIM
    frow = lax.broadcasted_iota(jnp.int32, (hd, 2 * t), 0)
    rest = jnp.full((hd, 2 * t), slope, F32)
    qf = jnp.zeros((hd, 2 * t), F32)
    for term in range(3):
        part = rest.astype(BF16).astype(F32)
        qf = jnp.where((frow >> 1) == term, part, qf)
        rest = rest - part
    qf = qf.astype(BF16)
    ones = jnp.ones((ones_rows, t), BF16)

    def block(qb):
        i = step * nqb + qb
        qt = qt_ref[qb]
        z = jnp.zeros_like(qt)
        qm = jnp.concatenate([jnp.where(top, qt, z), jnp.where(top, z, qt)], axis=1)
        qa = jnp.concatenate([qm, qf], axis=0)

        def tile_index(jj):
            return jnp.where(jj == 0, i, (i + jj) & (nk - 1))

        def scores(jj):
            j = tile_index(jj)
            above = j > i
            kt = k_ref[pl.ds(pl.multiple_of(j * t, t), t), :]
            ka = jnp.concatenate([kt, feat_ref[jnp.where(above, 1, 0)]], axis=1)
            s = _dot(ka, qa)
            if isinstance(jj, int):
                s = s + c_ref[...] if jj == 0 else s
            else:
                s = s + jnp.where(jj == 0, 1.0, 0.0) * c_ref[...]
            cpos = slope * ((j - i) * t - qloc).astype(F32)
            cvec = jnp.where(above, -cpos, cpos)
            return s, cvec

        def weighted_values(jj, s, ref):
            vta = jnp.concatenate([vt_ref[tile_index(jj)], ones], axis=0)
            return _dot(vta, jnp.exp2(s - ref).astype(BF16))

        def lagged_pass(ref0):
            s, cvec = scores(0)
            m = jnp.max(s, axis=0, keepdims=True) + cvec
            if ref0 is None:
                acc = weighted_values(0, s, m - cvec)
                excess = jnp.zeros_like(m)
            else:
                acc = jnp.exp2(ref0 - m) * weighted_values(0, s, ref0 - cvec)
                excess = jnp.abs(m - ref0)
            for jj in range(1, nk):
                s, cvec = scores(jj)
                r = weighted_values(jj, s, m - cvec)
                smax = jnp.max(s, axis=0, keepdims=True) + cvec
                m_new = jnp.maximum(m, smax)
                excess = jnp.maximum(excess, smax - m)
                acc = jnp.exp2(m - m_new) * (acc + r)
                m = m_new
            return m, acc, excess

        def per_tile_max_pass():
            def tile(jj, carry):
                m, acc = carry
                s, cvec = scores(jj)
                m_new = jnp.maximum(m, jnp.max(s, axis=0, keepdims=True) + cvec)
                return m_new, jnp.exp2(m - m_new) * acc + weighted_values(jj, s, m_new - cvec)

            init = (jnp.full((1, 2 * t), -jnp.inf, F32), jnp.zeros((hd + ones_rows, 2 * t), F32))
            return lax.fori_loop(0, nk, tile, init)[1]

        return lagged_pass, per_tile_max_pass

    def finish(qb, acc):
        inv1 = 1.0 / acc[hd:hd + 1, :t]
        inv2 = 1.0 / acc[hd:hd + 1, t:]
        o = acc[:hd, :t] * inv1 - lam * (acc[:hd, t:] * inv2)
        ms = jnp.mean(o * o, axis=0, keepdims=True)
        o = o * lax.rsqrt(ms + LN_EPS) * g_ref[...] * (1.0 - LAM_INIT_1)
        o_ref[pl.ds(pl.multiple_of(qb * t, t), t), :] = o.T.astype(BF16)

    ok = m = None
    for qb in range(nqb):
        m, acc, excess = block(qb)[0](m)
        finish(qb, acc)
        good = jnp.logical_and(jnp.sum(acc * 0.0) == 0.0, jnp.max(excess) <= LAG_MARGIN * LOG2E)
        ok = good if ok is None else jnp.logical_and(ok, good)

    @pl.when(jnp.logical_not(ok))
    def _():
        @pl.loop(0, nqb)
        def _(qb):
            finish(qb, block(qb)[1]())


def _diff_features(t):
    a = np.arange(t)
    lo = a % BF16_EXACT_INT
    f = np.zeros((t, 2 * DIFF_HEAD_DIM), np.float32)
    f[:, 0:6:2] = lo[:, None]
    f[:, 1:6:2] = (a - lo)[:, None]
    return jnp.asarray(np.stack([f, -f]), BF16)


def _diffattn(lam, slopes, qt, k, vt, g_col, batch, seq):
    t = TK_DIFF
    nk = seq // t
    assert nk & (nk - 1) == 0
    hd = 2 * DIFF_HEAD_DIM
    feat = _diff_features(t)
    smem = pl.BlockSpec(memory_space=pltpu.SMEM)
    nqb = QBLOCKS_DIFF
    steps = nk // nqb
    return pl.pallas_call(
        functools.partial(_diff_kernel, t=t, nk=nk, nqb=nqb),
        grid=(batch, DIFF_HEADS, steps),
        in_specs=[smem, smem,
                  pl.BlockSpec((nqb, hd, t), lambda b, h, i: (b * steps + i, h, 0)),
                  pl.BlockSpec((seq, hd), lambda b, h, i: (b, h)),
                  pl.BlockSpec((nk, hd, t), lambda b, h, i: (b, h, 0)),
                  _const_spec(feat.shape),
                  _const_spec(g_col.shape)],
        out_specs=pl.BlockSpec((nqb * t, hd), lambda b, h, i: (b * steps + i, h)),
        out_shape=jax.ShapeDtypeStruct(k.shape, BF16),
        scratch_shapes=[pltpu.VMEM((t, 2 * t), F32)],
        compiler_params=pltpu.CompilerParams(dimension_semantics=("arbitrary",) * 3,
                                             vmem_limit_bytes=VMEM_LIMIT),
        name="diffattn",
    )(lam, slopes, qt, k, vt, feat, g_col)


def kernel(x_prompt, x_sample, l0_w_in, l0_w_out, l0_gate_ln_g, l0_gate_ln_b, l0_w_spatial, l0_b_spatial, l0_na_rpb, l0_ln1_g, l0_ln1_b, l0_w_ff1, l0_w_ff2, l0_ln2_g, l0_ln2_b, l1_w_in, l1_w_out, l1_lambda_q1, l1_lambda_k1, l1_lambda_q2, l1_lambda_k2, l1_subln_g, l1_ln1_g, l1_ln1_b, l1_w_ff1, l1_w_ff2, l1_ln2_g, l1_ln2_b):
    row = lambda p: p.reshape(1, -1).astype(F32)
    w_in0 = l0_w_in.astype(BF16)
    w_out0 = l0_w_out.astype(BF16)
    w_s = l0_w_spatial.astype(BF16)
    b_s = jnp.broadcast_to(l0_b_spatial.astype(F32)[:, :, None], (A_GROUPS, CHUNK, LANES))
    na_bias = _na_bias_table(l0_na_rpb.astype(F32))
    wqt1 = l1_w_in[:, :D_MODEL].T.astype(BF16)
    wk1 = l1_w_in[:, D_MODEL:2 * D_MODEL].astype(BF16)
    wvt1 = l1_w_in[:, 2 * D_MODEL:].T.astype(BF16)
    w_out1 = l1_w_out.astype(BF16)
    lam1 = jnp.exp(jnp.sum((l1_lambda_q1 * l1_lambda_k1).astype(F32)))
    lam2 = jnp.exp(jnp.sum((l1_lambda_q2 * l1_lambda_k2).astype(F32)))
    lam = (lam1 - lam2 + LAM_INIT_1).reshape(1).astype(F32)
    slopes = jnp.exp2(-8.0 * jnp.arange(1, DIFF_HEADS + 1, dtype=F32) / DIFF_HEADS)
    g_col = l1_subln_g.astype(F32).reshape(-1, 1)
    ff0 = (l0_w_ff1.astype(BF16), l0_w_ff2.astype(BF16))
    ff1 = (l1_w_ff1.astype(BF16), l1_w_ff2.astype(BF16))

    def trunk(x):
        batch, seq, _ = x.shape
        x2d = x.reshape(batch * seq, D_MODEL)
        out_a, q, k, v = _proj0(x2d, w_in0, row(l0_gate_ln_g), row(l0_gate_ln_b), w_s, b_s)
        out_b = _natten(q, k, v, na_bias, batch, seq)
        x2d = _post([out_a, out_b], w_out0, x2d, row(l0_ln1_g), row(l0_ln1_b),
                    ff0[0], ff0[1], row(l0_ln2_g), row(l0_ln2_b))
        qt, k, vt = _proj1(x2d, wqt1, wk1, wvt1)
        o = _diffattn(lam, slopes, qt, k, vt, g_col, batch, seq)
        x2d = _post([o], w_out1, x2d, row(l1_ln1_g), row(l1_ln1_b),
                    ff1[0], ff1[1], row(l1_ln2_g), row(l1_ln2_b))
        return x2d.reshape(batch, seq, D_MODEL)

    return (trunk(x_prompt), trunk(x_sample))
```
